```python
import math
import jax, jax.numpy as jnp
from jax import lax
import numpy as np

D_MODEL = 1024
BATCH = 16
SEQ = 2048
DEPTH = 1
DEC_BATCH = 32
DEC_SEQ = 8
PAST_LEN = 16384
PAGE_SIZE = 128

N_HEADS = 8
HEAD_DIM = 64
V_DIM = 2 * HEAD_DIM
ATTN_QK = N_HEADS * 2 * HEAD_DIM
ATTN_V = N_HEADS * V_DIM
CONV_CH = D_MODEL // 2
CONV_K = 31
FFN_DIM = 2816
FFN_CONV_K = 3
IN_COLS = 2 * ATTN_QK + ATTN_V + 2 * CONV_CH + 2 * D_MODEL
Q_BLOCK = 128
SCALE = HEAD_DIM ** -0.5
EPS = 1e-6
LN_EPS = 1e-5

kernel_name = 'hybrid_diffattn_conformer_convffn_step'


def rms_norm(x, g):
    xf = x.astype(jnp.float32)
    y = xf * lax.rsqrt(jnp.mean(xf * xf, -1, keepdims=True) + EPS)
    return (y * g.astype(jnp.float32)).astype(x.dtype)


def layer_norm(x, g, b):
    xf = x.astype(jnp.float32)
    xc = xf - jnp.mean(xf, -1, keepdims=True)
    var = jnp.mean(xc * xc, -1, keepdims=True)
    return (xc * lax.rsqrt(var + LN_EPS) * g.astype(jnp.float32) + b.astype(jnp.float32)).astype(x.dtype)


def causal_dwconv(x_ext, w, b):
    y = lax.conv_general_dilated(x_ext, w.astype(x_ext.dtype)[:, None, :], window_strides=(1,), padding='VALID',
                                 dimension_numbers=('NWC', 'WIO', 'NWC'), feature_group_count=x_ext.shape[-1])
    return y + b.astype(y.dtype)


def diff_lambda(lq1, lk1, lq2, lk2, lam_init):
    f = jnp.float32
    return (jnp.exp(jnp.sum(lq1.astype(f) * lk1.astype(f)))
            - jnp.exp(jnp.sum(lq2.astype(f) * lk2.astype(f))) + lam_init)


def prompt_attention(q, k, v, lam):
    B, S = q.shape[0], q.shape[1]
    kf = k.astype(jnp.float32)
    vf = v.astype(jnp.float32)
    kpos = jnp.arange(S)

    def block(i):
        qb = lax.dynamic_slice_in_dim(q, i * Q_BLOCK, Q_BLOCK, axis=1).astype(jnp.float32)
        s = jnp.einsum('bqhcd,bkhcd->bhcqk', qb, kf) * SCALE
        qpos = i * Q_BLOCK + jnp.arange(Q_BLOCK)
        s = jnp.where(kpos[None, :] <= qpos[:, None], s, -jnp.inf)
        p = jax.nn.softmax(s, axis=-1)
        pd = p[:, :, 0] - lam * p[:, :, 1]
        return jnp.einsum('bhqk,bkhe->bqhe', pd, vf)

    out = lax.map(block, jnp.arange(S // Q_BLOCK))
    return out.transpose(1, 0, 2, 3, 4).reshape(B, S, N_HEADS, V_DIM)


def sample_attention(q, k, v, lam, cache_k, cache_v, page_table, layer):
    DB, T = q.shape[0], q.shape[1]
    qf = q.astype(jnp.float32) * SCALE

    def step(carry, pages):
        m, l, acc = carry
        kp = cache_k[layer, pages].astype(jnp.float32).reshape(DB, PAGE_SIZE, N_HEADS, 2, HEAD_DIM)
        vp = cache_v[layer, pages].astype(jnp.float32)
        s = jnp.einsum('bqhcd,bkhcd->bhcqk', qf, kp)
        m_new = jnp.maximum(m, s.max(-1))
        e = jnp.exp(s - m_new[..., None])
        corr = jnp.exp(m - m_new)
        l = l * corr + e.sum(-1)
        acc = acc * corr[..., None] + jnp.einsum('bhcqk,bkhe->bhcqe', e, vp)
        return (m_new, l, acc), None

    init = (jnp.full((DB, N_HEADS, 2, T), -jnp.inf, jnp.float32),
            jnp.zeros((DB, N_HEADS, 2, T), jnp.float32),
            jnp.zeros((DB, N_HEADS, 2, T, V_DIM), jnp.float32))
    (m, l, acc), _ = lax.scan(step, init, page_table.T)
    s = jnp.einsum('bqhcd,bkhcd->bhcqk', qf, k.astype(jnp.float32))
    tpos = jnp.arange(T)
    s = jnp.where(tpos[None, :] <= tpos[:, None], s, -jnp.inf)
    m_new = jnp.maximum(m, s.max(-1))
    e = jnp.exp(s - m_new[..., None])
    corr = jnp.exp(m - m_new)
    l = l * corr + e.sum(-1)
    acc = acc * corr[..., None] + jnp.einsum('bhcqk,bkhe->bhcqe', e, v.astype(jnp.float32))
    o = acc / l[..., None]
    od = o[:, :, 0] - lam * o[:, :, 1]
    return od.transpose(0, 2, 1, 3)


def trunk_layer(x, conv_hist, ffn_hist, attend, lam_init,
                norm_mix_pre, w_in, b_glu, b_gate, lambda_q1, lambda_k1, lambda_q2, lambda_k2, subln,
                w_a, conv_w, conv_b, ln_g, ln_b, w_b, w_out, norm_mix_post,
                norm_ffn_pre, w_up, ffn_conv_w, ffn_conv_b, w_down, norm_ffn_post):
    Bn, T = x.shape[0], x.shape[1]
    h = rms_norm(x, norm_mix_pre)
    z = h @ w_in
    q, k, v, glu, gates = jnp.split(z, [ATTN_QK, 2 * ATTN_QK, 2 * ATTN_QK + ATTN_V,
                                        2 * ATTN_QK + ATTN_V + 2 * CONV_CH], axis=-1)
    q = q.reshape(Bn, T, N_HEADS, 2, HEAD_DIM)
    k = k.reshape(Bn, T, N_HEADS, 2, HEAD_DIM)
    v = v.reshape(Bn, T, N_HEADS, V_DIM)
    lam = diff_lambda(lambda_q1, lambda_k1, lambda_q2, lambda_k2, lam_init)
    o = attend(q, k, v, lam)
    o = rms_norm(o, subln) * (1.0 - lam_init)
    y_a = o.reshape(Bn, T, ATTN_V).astype(x.dtype) @ w_a
    glu = glu + b_glu
    c = glu[..., :CONV_CH] * jax.nn.sigmoid(glu[..., CONV_CH:])
    c_ext = jnp.concatenate([conv_hist.astype(c.dtype), c], axis=1)
    c = jax.nn.silu(layer_norm(causal_dwconv(c_ext, conv_w, conv_b), ln_g, ln_b))
    y_b = c @ w_b
    g = jax.nn.sigmoid(gates + b_gate)
    mix = g[..., :D_MODEL] * y_a + g[..., D_MODEL:] * y_b
    x = x + rms_norm(mix @ w_out, norm_mix_post)
    h = rms_norm(x, norm_ffn_pre)
    gu = h @ w_up
    fg, fu = gu[..., :FFN_DIM], gu[..., FFN_DIM:]
    fg_ext = jnp.concatenate([ffn_hist.astype(fg.dtype), fg], axis=1)
    f = jax.nn.gelu(causal_dwconv(fg_ext, ffn_conv_w, ffn_conv_b)) * fu
    x = x + rms_norm(f @ w_down, norm_ffn_post)
    k_rows = k.reshape(Bn, T, N_HEADS, 2 * HEAD_DIM)
    return (x, k_rows, v, c_ext[:, -(CONV_K - 1):], fg_ext[:, -(FFN_CONV_K - 1):])


def setup_inputs(seed: int = 0) -> dict:
    key = jax.random.key(seed)
    ks = iter(jax.random.split(key, 40))

    def nrm(shape, scale):
        return jax.random.normal(next(ks), shape, jnp.float32) * scale

    n_pages = PAST_LEN // PAGE_SIZE
    n_pool = (DEC_BATCH * n_pages * 5) // 4
    L = DEPTH
    d = D_MODEL
    x_prompt = nrm((BATCH, SEQ, d), 1.0)
    x_sample = nrm((DEC_BATCH, DEC_SEQ, d), 1.0)
    cache_k = nrm((L, n_pool, PAGE_SIZE, N_HEADS, 2 * HEAD_DIM), 1.0)
    cache_v = nrm((L, n_pool, PAGE_SIZE, N_HEADS, V_DIM), 1.0)
    state_conv = nrm((L, DEC_BATCH, CONV_K - 1, CONV_CH), 0.5)
    state_ffn = nrm((L, DEC_BATCH, FFN_CONV_K - 1, FFN_DIM), 1.0)
    perm = jax.random.permutation(next(ks), n_pool)
    page_table = perm[:DEC_BATCH * n_pages].reshape(DEC_BATCH, n_pages).astype(jnp.int32)
    return {
        'x_prompt': x_prompt, 'x_sample': x_sample,
        'cache_k': cache_k, 'cache_v': cache_v,
        'state_conv': state_conv, 'state_ffn': state_ffn,
        'page_table': page_table,
        'norm_mix_pre': 1.0 + nrm((L, d), 0.05),
        'w_in': nrm((L, d, IN_COLS), d ** -0.5),
        'b_glu': nrm((L, 2 * CONV_CH), 0.02),
        'b_gate': nrm((L, 2 * d), 0.02),
        'lambda_q1': nrm((L, HEAD_DIM), 0.1),
        'lambda_k1': nrm((L, HEAD_DIM), 0.1),
        'lambda_q2': nrm((L, HEAD_DIM), 0.1),
        'lambda_k2': nrm((L, HEAD_DIM), 0.1),
        'subln': 1.0 + nrm((L, V_DIM), 0.05),
        'w_a': nrm((L, ATTN_V, d), ATTN_V ** -0.5),
        'conv_w': nrm((L, CONV_K, CONV_CH), CONV_K ** -0.5),
        'conv_b': nrm((L, CONV_CH), 0.02),
        'ln_g': 1.0 + nrm((L, CONV_CH), 0.05),
        'ln_b': nrm((L, CONV_CH), 0.02),
        'w_b': nrm((L, CONV_CH, d), CONV_CH ** -0.5),
        'w_out': nrm((L, d, d), d ** -0.5),
        'norm_mix_post': 1.0 + nrm((L, d), 0.05),
        'norm_ffn_pre': 1.0 + nrm((L, d), 0.05),
        'w_up': nrm((L, d, 2 * FFN_DIM), d ** -0.5),
        'ffn_conv_w': nrm((L, FFN_CONV_K, FFN_DIM), FFN_CONV_K ** -0.5),
        'ffn_conv_b': nrm((L, FFN_DIM), 0.02),
        'w_down': nrm((L, FFN_DIM, d), FFN_DIM ** -0.5),
        'norm_ffn_post': 1.0 + nrm((L, d), 0.05),
    }


def reference(x_prompt, x_sample, cache_k, cache_v, state_conv, state_ffn, page_table,
              norm_mix_pre, w_in, b_glu, b_gate, lambda_q1, lambda_k1, lambda_q2, lambda_k2, subln,
              w_a, conv_w, conv_b, ln_g, ln_b, w_b, w_out, norm_mix_post,
              norm_ffn_pre, w_up, ffn_conv_w, ffn_conv_b, w_down, norm_ffn_post):
    yp, ys = x_prompt, x_sample
    kp_l, vp_l, cp_l, fp_l, ks_l, vs_l, cs_l, fs_l = [], [], [], [], [], [], [], []
    for l in range(DEPTH):
        w = (norm_mix_pre[l], w_in[l], b_glu[l], b_gate[l], lambda_q1[l], lambda_k1[l], lambda_q2[l],
             lambda_k2[l], subln[l], w_a[l], conv_w[l], conv_b[l], ln_g[l], ln_b[l], w_b[l], w_out[l],
             norm_mix_post[l], norm_ffn_pre[l], w_up[l], ffn_conv_w[l], ffn_conv_b[l], w_down[l],
             norm_ffn_post[l])
        lam_init = 0.8 - 0.6 * math.exp(-0.3 * l)
        conv0 = jnp.zeros((yp.shape[0], CONV_K - 1, CONV_CH), yp.dtype)
        ffn0 = jnp.zeros((yp.shape[0], FFN_CONV_K - 1, FFN_DIM), yp.dtype)
        yp, kp, vp, cp, fp = trunk_layer(yp, conv0, ffn0, prompt_attention, lam_init, *w)
        attend_s = lambda q, k, v, lam, layer=l: sample_attention(q, k, v, lam, cache_k, cache_v, page_table, layer)
        ys, k_s, v_s, c_s, f_s = trunk_layer(ys, state_conv[l], state_ffn[l], attend_s, lam_init, *w)
        kp_l.append(kp); vp_l.append(vp); cp_l.append(cp); fp_l.append(fp)
        ks_l.append(k_s); vs_l.append(v_s); cs_l.append(c_s); fs_l.append(f_s)
    return (yp, ys, jnp.stack(kp_l), jnp.stack(vp_l), jnp.stack(cp_l), jnp.stack(fp_l),
            jnp.stack(ks_l), jnp.stack(vs_l), jnp.stack(cs_l), jnp.stack(fs_l))
```

```python
import functools
import math

import jax
import jax.numpy as jnp
from jax import lax
from jax.experimental import pallas as pl
from jax.experimental.pallas import tpu as pltpu

F32 = jnp.float32
BF16 = jnp.bfloat16

V7X_VMEM_BYTES = 64 * 1024 * 1024
LANES = 128
SUBLANES = 8

EPS = 1e-6
LN_EPS = 1e-5
NEG_BIG = -1e30

HIST_C_PAD = 32
HIST_F_PAD = 8


def _vmem_limit(estimate_bytes):
    return int(min(V7X_VMEM_BYTES - 4 * 1024 * 1024, max(estimate_bytes, 16 * 1024 * 1024)))


def _resident(shape):
    nd = len(shape)
    return pl.BlockSpec(shape, lambda *_: (0,) * nd, pipeline_mode=pl.Buffered(1))


def _rms(x, gain):
    return x * lax.rsqrt(jnp.mean(x * x, axis=-1, keepdims=True) + EPS) * gain


def _diff_lambda(lq1_ref, lk1_ref, lq2_ref, lk2_ref, lam_init):
    a = jnp.sum(lq1_ref[...] * lk1_ref[...], axis=-1, keepdims=True)
    b = jnp.sum(lq2_ref[...] * lk2_ref[...], axis=-1, keepdims=True)
    return jnp.exp(a) - jnp.exp(b) + lam_init


def _dot_nt(a, b):
    return lax.dot_general(a, b, (((1,), (1,)), ((), ())), preferred_element_type=F32)


def _proj_in_kernel(x_ref, gain_ref, w_ref, bglu_ref, bgate_ref,
                    q_ref, k_ref, v_ref, c_ref, g_ref, *, d_qk, d_v, d_c, d_g, scale):
    h = _rms(x_ref[...], gain_ref[...]).astype(BF16)

    def proj(c0, n):
        return jnp.dot(h, w_ref[:, c0:c0 + n], preferred_element_type=F32)

    q_ref[...] = (proj(0, d_qk) * scale).astype(q_ref.dtype)
    k_ref[...] = proj(d_qk, d_qk)
    v_ref[...] = proj(2 * d_qk, d_v)
    c0 = 2 * d_qk + d_v
    glu_a = proj(c0, d_c) + bglu_ref[:, :d_c]
    glu_b = proj(c0 + d_c, d_c) + bglu_ref[:, d_c:]
    c_ref[...] = glu_a * jax.nn.sigmoid(glu_b)
    c0 += 2 * d_c
    half = d_g // 2
    for i in range(2):
        gates = proj(c0 + i * half, half) + bgate_ref[:, i * half:(i + 1) * half]
        g_ref[:, i * half:(i + 1) * half] = jax.nn.sigmoid(gates).astype(g_ref.dtype)


def _proj_in(x2d, gain, w_in_bf, b_glu, b_gate, *, tm, q_dtype, d_qk, d_v, d_c, scale):
    m, d = x2d.shape
    n_in = w_in_bf.shape[1]
    d_g = n_in - 2 * d_qk - d_v - 2 * d_c
    row = lambda n: pl.BlockSpec((tm, n), lambda i: (i, 0))
    est = (d * n_in * 2 + 2 * tm * (d * 4 + d_qk * 4 * 2 + d_v * 4 + d_c * 4 + d_g * 2)
           + 6 * tm * max(d_qk, d_g // 2) * 4)
    return pl.pallas_call(
        functools.partial(_proj_in_kernel, d_qk=d_qk, d_v=d_v, d_c=d_c, d_g=d_g, scale=scale),
        grid=(m // tm,),
        in_specs=[row(d), _resident((1, d)), _resident((d, n_in)),
                  _resident((1, 2 * d_c)), _resident((1, d_g))],
        out_specs=[row(d_qk), row(d_qk), row(d_v), row(d_c), row(d_g)],
        out_shape=[jax.ShapeDtypeStruct((m, d_qk), q_dtype),
                   jax.ShapeDtypeStruct((m, d_qk), F32),
                   jax.ShapeDtypeStruct((m, d_v), F32),
                   jax.ShapeDtypeStruct((m, d_c), F32),
                   jax.ShapeDtypeStruct((m, d_g), BF16)],
        compiler_params=pltpu.CompilerParams(
            dimension_semantics=("parallel",), vmem_limit_bytes=_vmem_limit(est)),
        name="proj_in",
    )(x2d, gain, w_in_bf, b_glu, b_gate)


def _split_halves(qh, head_dim):
    lane = lax.broadcasted_iota(jnp.int32, qh.shape, 1)
    zero = jnp.zeros_like(qh)
    return jnp.concatenate([jnp.where(lane < head_dim, qh, zero),
                            jnp.where(lane >= head_dim, qh, zero)], axis=0)


def _sub_norm(o, subln, lam_init):
    return _rms(o, subln) * (1.0 - lam_init)


def _prompt_attn_kernel(lq1_ref, lk1_ref, lq2_ref, lk2_ref, subln_ref, q_ref, k_ref, v_ref, o_ref,
                        *, tq, head_dim, lam_init):
    qi = pl.program_id(2)
    lam = _diff_lambda(lq1_ref, lk1_ref, lq2_ref, lk2_ref, lam_init)
    qs = _split_halves(q_ref[...].astype(F32), head_dim).astype(BF16)
    dv = v_ref.shape[-1]

    def step(j, carry, masked):
        m, l, acc = carry
        start = pl.multiple_of(j * tq, tq)
        kb = k_ref[pl.ds(start, tq), :].astype(BF16)
        vb = v_ref[pl.ds(start, tq), :].astype(BF16)
        s = _dot_nt(qs, kb)
        if masked:
            r = lax.broadcasted_iota(jnp.int32, s.shape, 0)
            c = lax.broadcasted_iota(jnp.int32, s.shape, 1)
            r = jnp.where(r >= tq, r - tq, r)
            s = jnp.where(c <= r, s, -jnp.inf)
        m_new = jnp.maximum(m, jnp.max(s, axis=-1, keepdims=True))
        e = jnp.exp(s - m_new)
        corr = jnp.exp(m - m_new)
        l = l * corr + jnp.sum(e, axis=-1, keepdims=True)
        acc = acc * corr + jnp.dot(e.astype(BF16), vb, preferred_element_type=F32)
        return m_new, l, acc

    init = (jnp.full((2 * tq, 1), NEG_BIG, F32), jnp.zeros((2 * tq, 1), F32),
            jnp.zeros((2 * tq, dv), F32))
    carry = lax.fori_loop(0, qi, lambda j, c: step(j, c, False), init)
    _, l, acc = step(qi, carry, True)
    o = acc / l
    od = o[:tq] - lam * o[tq:]
    o_ref[...] = _sub_norm(od, subln_ref[...], lam_init).astype(o_ref.dtype)


def _prompt_attention(q, k, v, lam_params, subln, *, n_heads, head_dim, tq, lam_init):
    b, s, _ = q.shape
    dv = v.shape[-1] // n_heads
    dqk = 2 * head_dim
    small = lambda n: pl.BlockSpec((1, n), lambda bi, h, i: (0, 0))
    est = 2 * s * (dqk + dv) * 4 * 2 + 12 * (2 * tq) * tq * 4
    return pl.pallas_call(
        functools.partial(_prompt_attn_kernel, tq=tq, head_dim=head_dim, lam_init=lam_init),
        grid=(b, n_heads, s // tq),
        in_specs=[small(head_dim)] * 4 + [small(dv),
                  pl.BlockSpec((None, tq, dqk), lambda bi, h, i: (bi, i, h)),
                  pl.BlockSpec((None, s, dqk), lambda bi, h, i: (bi, 0, h)),
                  pl.BlockSpec((None, s, dv), lambda bi, h, i: (bi, 0, h))],
        out_specs=pl.BlockSpec((None, tq, dv), lambda bi, h, i: (bi, i, h)),
        out_shape=jax.ShapeDtypeStruct((b, s, n_heads * dv), BF16),
        compiler_params=pltpu.CompilerParams(
            dimension_semantics=("parallel", "parallel", "arbitrary"),
            vmem_limit_bytes=_vmem_limit(est)),
        name="prompt_attn",
    )(*lam_params, subln, q, k, v)


def _sample_attn_kernel(pt_ref, lq1_ref, lk1_ref, lq2_ref, lk2_ref, subln_ref, q_ref, kn_ref, vn_ref,
                        *rest, pp, n_heads, head_dim, page, lam_init):
    del pt_ref
    k_refs, v_refs = rest[:pp], rest[pp:2 * pp]
    o_ref, m_ref, l_ref, acc_ref = rest[2 * pp:]
    pg = pl.program_id(1)
    t = q_ref.shape[0]
    dqk = 2 * head_dim
    dv = vn_ref.shape[-1] // n_heads

    @pl.when(pg == 0)
    def _():
        m_ref[...] = jnp.full(m_ref.shape, NEG_BIG, F32)
        l_ref[...] = jnp.zeros(l_ref.shape, F32)
        acc_ref[...] = jnp.zeros(acc_ref.shape, F32)

    def q_rows(h):
        return _split_halves(q_ref[:, h * dqk:(h + 1) * dqk], head_dim).astype(BF16)

    def update(h, s, vh):
        m_old = m_ref[h]
        m_new = jnp.maximum(m_old, jnp.max(s, axis=-1, keepdims=True))
        e = jnp.exp(s - m_new)
        corr = jnp.exp(m_old - m_new)
        l_ref[h] = l_ref[h] * corr + jnp.sum(e, axis=-1, keepdims=True)
        acc_ref[h] = acc_ref[h] * corr + jnp.dot(e.astype(BF16), vh, preferred_element_type=F32)
        m_ref[h] = m_new

    for h in range(n_heads):
        kh = jnp.concatenate([r[pl.ds(h, page, stride=n_heads), :] for r in k_refs], axis=0)
        vh = jnp.concatenate([r[pl.ds(h, page, stride=n_heads), :] for r in v_refs], axis=0)
        update(h, _dot_nt(q_rows(h), kh.astype(BF16)), vh.astype(BF16))

    @pl.when(pg == pl.num_programs(1) - 1)
    def _():
        lam = _diff_lambda(lq1_ref, lk1_ref, lq2_ref, lk2_ref, lam_init)
        pad = jnp.zeros((LANES - t, dqk), F32)
        r = lax.broadcasted_iota(jnp.int32, (2 * t, LANES), 0)
        c = lax.broadcasted_iota(jnp.int32, (2 * t, LANES), 1)
        r = jnp.where(r >= t, r - t, r)
        for h in range(n_heads):
            kn = jnp.concatenate([kn_ref[:, h * dqk:(h + 1) * dqk], pad], axis=0).astype(BF16)
            vn = jnp.concatenate([vn_ref[:, h * dv:(h + 1) * dv], pad], axis=0).astype(BF16)
            s = jnp.where(c <= r, _dot_nt(q_rows(h), kn), -jnp.inf)
            update(h, s, vn)
            o = acc_ref[h] / l_ref[h]
            od = o[:t] - lam * o[t:]
            o_ref[:, h * dv:(h + 1) * dv] = _sub_norm(od, subln_ref[...], lam_init)


def _sample_attention(q, k_new, v_new, cache_k2, cache_v2, page_table, layer_base, lam_params, subln,
                      *, n_heads, head_dim, page, pp, lam_init):
    db, t, _ = q.shape
    n_pages = page_table.shape[1]
    dqk = 2 * head_dim
    dv = v_new.shape[-1] // n_heads
    rows = page * n_heads
    small = lambda n: pl.BlockSpec((1, n), lambda b, p, pt: (0, 0))
    per_b = lambda n: pl.BlockSpec((None, t, n), lambda b, p, pt: (b, 0, 0))

    def page_spec(i):
        return pl.BlockSpec((None, rows, LANES), lambda b, p, pt: (layer_base + pt[b, p * pp + i], 0, 0))

    est = 2 * 2 * pp * rows * LANES * 4 + 8 * page * pp * LANES * 4 * 2
    return pl.pallas_call(
        functools.partial(_sample_attn_kernel, pp=pp, n_heads=n_heads, head_dim=head_dim, page=page,
                          lam_init=lam_init),
        grid_spec=pltpu.PrefetchScalarGridSpec(
            num_scalar_prefetch=1,
            grid=(db, n_pages // pp),
            in_specs=[small(head_dim)] * 4 + [small(dv), per_b(n_heads * dqk), per_b(n_heads * dqk),
                                              per_b(n_heads * dv)]
                     + [page_spec(i) for i in range(pp)] * 2,
            out_specs=per_b(n_heads * dv),
            scratch_shapes=[pltpu.VMEM((n_heads, 2 * t, 1), F32), pltpu.VMEM((n_heads, 2 * t, 1), F32),
                            pltpu.VMEM((n_heads, 2 * t, dv), F32)]),
        out_shape=jax.ShapeDtypeStruct((db, t, n_heads * dv), F32),
        compiler_params=pltpu.CompilerParams(
            dimension_semantics=("parallel", "arbitrary"), vmem_limit_bytes=_vmem_limit(est)),
        name="sample_attn",
    )(page_table, *lam_params, subln, q, k_new, v_new, *([cache_k2] * pp), *([cache_v2] * pp))


def _post_kernel(x_ref, o_ref, c_ref, g_ref, hc_ref, hf_ref,
                 wa_ref, cw_ref, cb_ref, lng_ref, lnb_ref, wb_ref, wout_ref, nmp_ref,
                 nfp_ref, wup_ref, fcw_ref, fcb_ref, wdown_ref, nfq_ref,
                 y_ref, cs_ref, fs_ref, cext_ref, fext_ref, *, nb, t, ffn_chunk):
    ti = pl.program_id(1)
    rows = nb * t
    d = x_ref.shape[-1]
    cch = c_ref.shape[-1]
    ffn = fext_ref.shape[-1]
    conv_k = cw_ref.shape[0]
    ffn_k = fcw_ref.shape[0]

    @pl.when(ti == 0)
    def _():
        cext_ref[:, 0:HIST_C_PAD, :] = hc_ref[...]
        fext_ref[:, 0:HIST_F_PAD, :] = hf_ref[...]

    cext_ref[:, HIST_C_PAD:HIST_C_PAD + t, :] = c_ref[...]
    base = HIST_C_PAD - (conv_k - 1)
    conv = jnp.zeros((nb, t, cch), F32) + cb_ref[...]
    for j in range(conv_k):
        conv = conv + cext_ref[:, pl.ds(base + j, t), :] * cw_ref[pl.ds(j, 1), :]
    mu = jnp.mean(conv, axis=-1, keepdims=True)
    xc = conv - mu
    var = jnp.mean(xc * xc, axis=-1, keepdims=True)
    ln = xc * lax.rsqrt(var + LN_EPS) * lng_ref[...] + lnb_ref[...]
    cact = jax.nn.silu(ln).reshape(rows, cch).astype(BF16)
    y_b = jnp.dot(cact, wb_ref[...], preferred_element_type=F32)
    y_a = jnp.dot(o_ref[...].astype(BF16), wa_ref[...], preferred_element_type=F32)
    mix = g_ref[:, :d].astype(F32) * y_a + g_ref[:, d:].astype(F32) * y_b
    x1 = x_ref[...] + _rms(jnp.dot(mix.astype(BF16), wout_ref[...], preferred_element_type=F32),
                           nmp_ref[...])

    h2 = _rms(x1, nfp_ref[...]).astype(BF16)
    fbase = HIST_F_PAD - (ffn_k - 1)
    acc = jnp.zeros((rows, d), F32)
    for c0 in range(0, ffn, ffn_chunk):
        cs = slice(c0, c0 + ffn_chunk)
        fg = jnp.dot(h2, wup_ref[:, cs], preferred_element_type=F32)
        fu = jnp.dot(h2, wup_ref[:, ffn + c0:ffn + c0 + ffn_chunk], preferred_element_type=F32)
        fext_ref[:, HIST_F_PAD:HIST_F_PAD + t, cs] = fg.reshape(nb, t, ffn_chunk)
        cv = jnp.zeros((nb, t, ffn_chunk), F32) + fcb_ref[:, cs]
        for j in range(ffn_k):
            cv = cv + fext_ref[:, pl.ds(fbase + j, t), cs] * fcw_ref[pl.ds(j, 1), cs]
        f = jax.nn.gelu(cv).reshape(rows, ffn_chunk) * fu
        acc = acc + jnp.dot(f.astype(BF16), wdown_ref[cs, :], preferred_element_type=F32)
    y_ref[...] = x1 + _rms(acc, nfq_ref[...])

    c_tail = cext_ref[:, t:t + HIST_C_PAD, :]
    f_tail = fext_ref[:, t:t + HIST_F_PAD, :]
    cs_ref[...] = c_tail
    fs_ref[...] = f_tail
    cext_ref[:, 0:HIST_C_PAD, :] = c_tail
    fext_ref[:, 0:HIST_F_PAD, :] = f_tail


def _post(x2d, o2d, c3d, g2d, hist_c, hist_f, w, *, nb, t, ffn_chunk):
    n_seq, seq, cch = c3d.shape
    d = x2d.shape[1]
    dv = o2d.shape[1]
    ffn = w["w_down"].shape[0]
    tiles = seq // t
    rows = nb * t
    row2 = lambda n: pl.BlockSpec((rows, n), lambda b, i: (b * tiles + i, 0))
    seq3 = lambda r, n: pl.BlockSpec((nb, r, n), lambda b, i: (b, 0, 0))
    weights = [w["w_a"], w["conv_w"], w["conv_b"], w["ln_g"], w["ln_b"], w["w_b"], w["w_out"],
               w["norm_mix_post"], w["norm_ffn_pre"], w["w_up"], w["ffn_conv_w"], w["ffn_conv_b"],
               w["w_down"], w["norm_ffn_post"]]
    w_bytes = sum(int(a.size) * a.dtype.itemsize for a in weights)
    est = (w_bytes + 2 * rows * (d * 4 * 2 + dv * o2d.dtype.itemsize + cch * 4 + 2 * d * 2)
           + nb * ((HIST_C_PAD + t) * cch + (HIST_F_PAD + t) * ffn) * 4
           + 10 * rows * d * 4 + 8 * rows * ffn_chunk * 4)
    return pl.pallas_call(
        functools.partial(_post_kernel, nb=nb, t=t, ffn_chunk=ffn_chunk),
        grid=(n_seq // nb, tiles),
        in_specs=[row2(d), row2(dv), pl.BlockSpec((nb, t, cch), lambda b, i: (b, i, 0)), row2(2 * d),
                  seq3(HIST_C_PAD, cch), seq3(HIST_F_PAD, ffn)] + [_resident(a.shape) for a in weights],
        out_specs=[row2(d), seq3(HIST_C_PAD, cch), seq3(HIST_F_PAD, ffn)],
        out_shape=[jax.ShapeDtypeStruct(x2d.shape, F32),
                   jax.ShapeDtypeStruct((n_seq, HIST_C_PAD, cch), F32),
                   jax.ShapeDtypeStruct((n_seq, HIST_F_PAD, ffn), F32)],
        scratch_shapes=[pltpu.VMEM((nb, HIST_C_PAD + t, cch), F32),
                        pltpu.VMEM((nb, HIST_F_PAD + t, ffn), F32)],
        compiler_params=pltpu.CompilerParams(
            dimension_semantics=("parallel", "arbitrary"), vmem_limit_bytes=_vmem_limit(est)),
        name="post",
    )(x2d, o2d, c3d, g2d, hist_c, hist_f, *weights)


def _pad_history(hist, rows):
    return jnp.pad(hist, ((0, 0), (rows - hist.shape[1], 0), (0, 0)))


def kernel(x_prompt, x_sample, cache_k, cache_v, state_conv, state_ffn, page_table, norm_mix_pre, w_in, b_glu, b_gate, lambda_q1, lambda_k1, lambda_q2, lambda_k2, subln, w_a, conv_w, conv_b, ln_g, ln_b, w_b, w_out, norm_mix_post, norm_ffn_pre, w_up, ffn_conv_w, ffn_conv_b, w_down, norm_ffn_post):
    depth = w_in.shape[0]
    bp, sp, d = x_prompt.shape
    bs, ts, _ = x_sample.shape
    _, n_pool, page, n_heads, dqk = cache_k.shape
    head_dim = dqk // 2
    dv = cache_v.shape[-1]
    d_qk, d_v = n_heads * dqk, n_heads * dv
    cch = conv_w.shape[-1]
    conv_k = conv_w.shape[1]
    ffn = w_down.shape[1]
    ffn_k = ffn_conv_w.shape[1]
    scale = head_dim ** -0.5

    cache_k2 = cache_k.reshape(depth * n_pool, page * n_heads, dqk)
    cache_v2 = cache_v.reshape(depth * n_pool, page * n_heads, dv)

    yp = x_prompt.reshape(bp * sp, d)
    ys = x_sample.reshape(bs * ts, d)
    zeros_c = jnp.zeros((bp, HIST_C_PAD, cch), F32)
    zeros_f = jnp.zeros((bp, HIST_F_PAD, ffn), F32)
    outs = [[] for _ in range(8)]
    for l in range(depth):
        lam_init = 0.8 - 0.6 * math.exp(-0.3 * l)
        row = lambda a: a[l].reshape(1, -1)
        lam_params = [row(lambda_q1), row(lambda_k1), row(lambda_q2), row(lambda_k2)]
        w_in_bf = w_in[l].astype(BF16)
        proj_args = (row(norm_mix_pre), w_in_bf, row(b_glu), row(b_gate))
        proj_kw = dict(d_qk=d_qk, d_v=d_v, d_c=cch, scale=scale)
        w = {
            "w_a": w_a[l].astype(BF16), "conv_w": conv_w[l], "conv_b": row(conv_b), "ln_g": row(ln_g),
            "ln_b": row(ln_b), "w_b": w_b[l].astype(BF16), "w_out": w_out[l].astype(BF16),
            "norm_mix_post": row(norm_mix_post), "norm_ffn_pre": row(norm_ffn_pre),
            "w_up": w_up[l].astype(BF16), "ffn_conv_w": ffn_conv_w[l], "ffn_conv_b": row(ffn_conv_b),
            "w_down": w_down[l].astype(BF16), "norm_ffn_post": row(norm_ffn_post),
        }

        q, k, v, c, g = _proj_in(yp, *proj_args, tm=512, q_dtype=BF16, **proj_kw)
        o = _prompt_attention(q.reshape(bp, sp, d_qk), k.reshape(bp, sp, d_qk), v.reshape(bp, sp, d_v),
                              lam_params, row(subln), n_heads=n_heads, head_dim=head_dim, tq=256,
                              lam_init=lam_init)
        yp, cs, fs = _post(yp, o.reshape(bp * sp, d_v), c.reshape(bp, sp, cch), g, zeros_c, zeros_f, w,
                           nb=1, t=256, ffn_chunk=256)
        outs[0].append(k.reshape(bp, sp, n_heads, dqk))
        outs[1].append(v.reshape(bp, sp, n_heads, dv))
        outs[2].append(cs[:, HIST_C_PAD - (conv_k - 1):])
        outs[3].append(fs[:, HIST_F_PAD - (ffn_k - 1):])

        q, k, v, c, g = _proj_in(ys, *proj_args, tm=bs * ts, q_dtype=F32, **proj_kw)
        o = _sample_attention(q.reshape(bs, ts, d_qk), k.reshape(bs, ts, d_qk), v.reshape(bs, ts, d_v),
                              cache_k2, cache_v2, page_table, l * n_pool, lam_params, row(subln),
                              n_heads=n_heads, head_dim=head_dim, page=page, pp=8, lam_init=lam_init)
        ys, cs, fs = _post(ys, o.reshape(bs * ts, d_v), c.reshape(bs, ts, cch), g,
                           _pad_history(state_conv[l], HIST_C_PAD), _pad_history(state_ffn[l], HIST_F_PAD),
                           w, nb=bs, t=ts, ffn_chunk=256)
        outs[4].append(k.reshape(bs, ts, n_heads, dqk))
        outs[5].append(v.reshape(bs, ts, n_heads, dv))
        outs[6].append(cs[:, HIST_C_PAD - (conv_k - 1):])
        outs[7].append(fs[:, HIST_F_PAD - (ffn_k - 1):])

    return (yp.reshape(bp, sp, d), ys.reshape(bs, ts, d)) + tuple(jnp.stack(o) for o in outs)
```

```python
import functools
import math

import jax
import jax.numpy as jnp
from jax import lax
from jax.experimental import pallas as pl
from jax.experimental.pallas import tpu as pltpu

F32 = jnp.float32
BF16 = jnp.bfloat16

V7X_VMEM_BYTES = 64 * 1024 * 1024
LANES = 128
SUBLANES = 8

EPS = 1e-6
LN_EPS = 1e-5
NEG_BIG = -1e30

HIST_C_PAD = 32
HIST_F_PAD = 8
CONV_ROW_CHUNK = 64


def _vmem_limit(estimate_bytes):
    return int(min(V7X_VMEM_BYTES - 4 * 1024 * 1024, max(estimate_bytes, 16 * 1024 * 1024)))


def _resident(shape):
    nd = len(shape)
    return pl.BlockSpec(shape, lambda *_: (0,) * nd, pipeline_mode=pl.Buffered(1))


def _rms(x, gain):
    return x * lax.rsqrt(jnp.mean(x * x, axis=-1, keepdims=True) + EPS) * gain


def _diff_lambda(lq1_ref, lk1_ref, lq2_ref, lk2_ref, lam_init):
    a = jnp.sum(lq1_ref[...] * lk1_ref[...], axis=-1, keepdims=True)
    b = jnp.sum(lq2_ref[...] * lk2_ref[...], axis=-1, keepdims=True)
    return jnp.exp(a) - jnp.exp(b) + lam_init


def _dot_nt(a, b):
    return lax.dot_general(a, b, (((1,), (1,)), ((), ())), preferred_element_type=F32)


def _split_halves(qh, head_dim):
    lane = lax.broadcasted_iota(jnp.int32, qh.shape, 1)
    zero = jnp.zeros_like(qh)
    return jnp.concatenate([jnp.where(lane < head_dim, qh, zero),
                            jnp.where(lane >= head_dim, qh, zero)], axis=0)


def _sub_norm(o, subln, lam_init):
    return _rms(o, subln) * (1.0 - lam_init)


def _causal_dwconv(ext_ref, w_ref, bias, *, t, pad, cols=slice(None), row_chunk=None):
    k = w_ref.shape[0]
    base = pad - (k - 1)
    row_chunk = row_chunk or t
    outs = []
    for c0 in range(0, t, row_chunk):
        n = min(row_chunk, t - c0)
        y = None
        for r in range(SUBLANES):
            taps = [j for j in range(k) if (base + j) % SUBLANES == r]
            if not taps:
                continue
            span = n if r == 0 else n + SUBLANES
            u = None
            for j in taps:
                a0 = c0 + base + j - r
                term = ext_ref[:, a0:a0 + span, cols] * w_ref[pl.ds(j, 1), cols]
                u = term if u is None else u + term
            u = u if r == 0 else u[:, r:r + n, :]
            y = u if y is None else y + u
        outs.append(y + bias)
    return outs[0] if len(outs) == 1 else jnp.concatenate(outs, axis=1)


def _proj_in_kernel(x_ref, gain_ref, w_ref, bglu_ref, bgate_ref, hc_ref, cw_ref, cb_ref, lng_ref, lnb_ref,
                    q_ref, k_ref, v_ref, g_ref, ca_ref, cs_ref, cext_ref, *, nb, t, d_qk, d_v, scale,
                    conv_rows):
    ti = pl.program_id(1)
    rows = nb * t
    d_c = ca_ref.shape[-1]
    d_g = g_ref.shape[-1]
    h = _rms(x_ref[...], gain_ref[...]).astype(BF16)

    def proj(c0, n):
        return jnp.dot(h, w_ref[:, c0:c0 + n], preferred_element_type=F32)

    @pl.when(ti == 0)
    def _():
        cext_ref[:, 0:HIST_C_PAD, :] = hc_ref[...]

    c0 = 2 * d_qk + d_v
    glu_a = proj(c0, d_c) + bglu_ref[:, :d_c]
    glu_b = proj(c0 + d_c, d_c) + bglu_ref[:, d_c:]
    cext_ref[:, HIST_C_PAD:HIST_C_PAD + t, :] = (glu_a * jax.nn.sigmoid(glu_b)).reshape(nb, t, d_c)

    q_ref[...] = (proj(0, d_qk) * scale).astype(q_ref.dtype)
    k_ref[...] = proj(d_qk, d_qk)
    v_ref[...] = proj(2 * d_qk, d_v)
    c0 += 2 * d_c
    half = d_g // 2
    for i in range(2):
        gates = proj(c0 + i * half, half) + bgate_ref[:, i * half:(i + 1) * half]
        g_ref[:, i * half:(i + 1) * half] = jax.nn.sigmoid(gates).astype(g_ref.dtype)

    conv = _causal_dwconv(cext_ref, cw_ref, cb_ref[...], t=t, pad=HIST_C_PAD, row_chunk=conv_rows)
    mu = jnp.mean(conv, axis=-1, keepdims=True)
    xc = conv - mu
    var = jnp.mean(xc * xc, axis=-1, keepdims=True)
    ln = xc * lax.rsqrt(var + LN_EPS) * lng_ref[...] + lnb_ref[...]
    ca_ref[...] = jax.nn.silu(ln).reshape(rows, d_c).astype(ca_ref.dtype)

    tail = cext_ref[:, t:t + HIST_C_PAD, :]
    cs_ref[...] = tail
    cext_ref[:, 0:HIST_C_PAD, :] = tail


def _proj_in(x2d, hist_c, w, *, n_seq, nb, t, q_dtype, d_qk, d_v, scale):
    m, d = x2d.shape
    tiles = (m // n_seq) // t
    rows = nb * t
    n_in = w["w_in"].shape[1]
    d_c = w["conv_w"].shape[1]
    d_g = n_in - 2 * d_qk - d_v - 2 * d_c
    row = lambda n: pl.BlockSpec((rows, n), lambda b, i: (b * tiles + i, 0))
    hist = pl.BlockSpec((nb, HIST_C_PAD, d_c), lambda b, i: (b, 0, 0))
    weights = [w["norm_mix_pre"], w["w_in"], w["b_glu"], w["b_gate"]]
    conv_weights = [w["conv_w"], w["conv_b"], w["ln_g"], w["ln_b"]]
    est = (d * n_in * 2 + 2 * rows * (d * 4 + d_qk * 8 + d_v * 4 + d_c * 2 + d_g * 2)
           + nb * (HIST_C_PAD + t) * d_c * 4 + 8 * rows * max(d_qk, d_g // 2) * 4)
    return pl.pallas_call(
        functools.partial(_proj_in_kernel, nb=nb, t=t, d_qk=d_qk, d_v=d_v, scale=scale,
                          conv_rows=min(t, CONV_ROW_CHUNK)),
        grid=(n_seq // nb, tiles),
        in_specs=([row(d)] + [_resident(a.shape) for a in weights] + [hist]
                  + [_resident(a.shape) for a in conv_weights]),
        out_specs=[row(d_qk), row(d_qk), row(d_v), row(d_g), row(d_c), hist],
        out_shape=[jax.ShapeDtypeStruct((m, d_qk), q_dtype), jax.ShapeDtypeStruct((m, d_qk), F32),
                   jax.ShapeDtypeStruct((m, d_v), F32), jax.ShapeDtypeStruct((m, d_g), BF16),
                   jax.ShapeDtypeStruct((m, d_c), BF16),
                   jax.ShapeDtypeStruct((n_seq, HIST_C_PAD, d_c), F32)],
        scratch_shapes=[pltpu.VMEM((nb, HIST_C_PAD + t, d_c), F32)],
        compiler_params=pltpu.CompilerParams(
            dimension_semantics=("parallel", "arbitrary"), vmem_limit_bytes=_vmem_limit(est)),
        name="proj_in",
    )(x2d, *weights, hist_c, *conv_weights)


def _prompt_attn_kernel(lq1_ref, lk1_ref, lq2_ref, lk2_ref, subln_ref, q_ref, k_ref, v_ref, o_ref, vt_ref,
                        *, tq, head_dim, heads, lam_init):
    qi = pl.program_id(2)
    dqk = 2 * head_dim
    dv = vt_ref.shape[1]
    n_tiles = k_ref.shape[0] // tq

    @pl.when(qi == 0)
    def _():
        eye = (lax.broadcasted_iota(jnp.int32, (dv, dv), 0)
               == lax.broadcasted_iota(jnp.int32, (dv, dv), 1)).astype(BF16)
        for hh in range(heads):
            vt_ref[hh] = _dot_nt(eye, v_ref[:, hh * dv:(hh + 1) * dv].astype(BF16)).astype(BF16)

    lam = _diff_lambda(lq1_ref, lk1_ref, lq2_ref, lk2_ref, lam_init)

    def attend(hh, n):
        qs = _split_halves(q_ref[:, hh * dqk:(hh + 1) * dqk], head_dim)
        scores, m = [], None
        for j in range(n):
            kb = k_ref[j * tq:(j + 1) * tq, hh * dqk:(hh + 1) * dqk].astype(BF16)
            s = _dot_nt(kb, qs)
            if j == n - 1:
                key = lax.broadcasted_iota(jnp.int32, s.shape, 0)
                qry = lax.broadcasted_iota(jnp.int32, s.shape, 1)
                qry = jnp.where(qry >= tq, qry - tq, qry)
                s = jnp.where(key <= qry, s, -jnp.inf)
            mj = jnp.max(s, axis=0, keepdims=True)
            m = mj if m is None else jnp.maximum(m, mj)
            scores.append(s)
        l = jnp.zeros((1, 2 * tq), F32)
        acc = jnp.zeros((dv, 2 * tq), F32)
        for j, s in enumerate(scores):
            e = jnp.exp(s - m)
            l = l + jnp.sum(e, axis=0, keepdims=True)
            acc = acc + jnp.dot(vt_ref[hh, :, j * tq:(j + 1) * tq], e.astype(BF16),
                                preferred_element_type=F32)
        o = acc / l
        od = (o[:, :tq] - lam * o[:, tq:]).T
        o_ref[:, hh * dv:(hh + 1) * dv] = _sub_norm(od, subln_ref[...], lam_init).astype(o_ref.dtype)

    for n in range(1, n_tiles + 1):
        @pl.when(qi == n - 1)
        def _(n=n):
            for hh in range(heads):
                attend(hh, n)


def _prompt_attention(q, k, v, lam_params, subln, *, n_heads, head_dim, tq, heads, lam_init):
    b, s, _ = q.shape
    dv = v.shape[-1] // n_heads
    dqk = 2 * head_dim
    small = lambda n: pl.BlockSpec((1, n), lambda bi, h, i: (0, 0))
    est = (2 * heads * (s * (dqk + dv) * 4 + tq * (dqk + dv) * 2) + heads * dv * s * 2
           + heads * s * 2 * tq * 4 * 2)
    return pl.pallas_call(
        functools.partial(_prompt_attn_kernel, tq=tq, head_dim=head_dim, heads=heads, lam_init=lam_init),
        grid=(b, n_heads // heads, s // tq),
        in_specs=[small(head_dim)] * 4 + [small(dv),
                  pl.BlockSpec((None, tq, heads * dqk), lambda bi, h, i: (bi, i, h)),
                  pl.BlockSpec((None, s, heads * dqk), lambda bi, h, i: (bi, 0, h)),
                  pl.BlockSpec((None, s, heads * dv), lambda bi, h, i: (bi, 0, h))],
        out_specs=pl.BlockSpec((None, tq, heads * dv), lambda bi, h, i: (bi, i, h)),
        out_shape=jax.ShapeDtypeStruct((b, s, n_heads * dv), BF16),
        scratch_shapes=[pltpu.VMEM((heads, dv, s), BF16)],
        compiler_params=pltpu.CompilerParams(
            dimension_semantics=("parallel", "parallel", "arbitrary"),
            vmem_limit_bytes=_vmem_limit(est)),
        name="prompt_attn",
    )(*lam_params, subln, q, k, v)


def _sample_attn_kernel(pt_ref, lq1_ref, lk1_ref, lq2_ref, lk2_ref, subln_ref, q_ref, kn_ref, vn_ref,
                        *rest, pp, n_heads, head_dim, lam_init):
    del pt_ref
    k_refs, v_refs = rest[:pp], rest[pp:2 * pp]
    o_ref, qall_ref, m_ref, l_ref, acc_ref = rest[2 * pp:]
    pg = pl.program_id(1)
    t = q_ref.shape[0]
    dqk = 2 * head_dim
    dv = acc_ref.shape[-1]
    n_rows = n_heads * 2 * t

    @pl.when(pg == 0)
    def _():
        qall_ref[...] = jnp.concatenate(
            [_split_halves(q_ref[:, h * dqk:(h + 1) * dqk], head_dim) for h in range(n_heads)],
            axis=0).astype(BF16)
        m_ref[...] = jnp.full(m_ref.shape, NEG_BIG, F32)
        l_ref[...] = jnp.zeros(l_ref.shape, F32)
        acc_ref[...] = jnp.zeros(acc_ref.shape, F32)

    qall = qall_ref[...]
    row = lax.broadcasted_iota(jnp.int32, (n_rows, LANES), 0)
    col = lax.broadcasted_iota(jnp.int32, (n_rows, LANES), 1)
    same_head = (col % n_heads) == (row // (2 * t))

    def update(scores, values):
        m_old = m_ref[...]
        m_new = m_old
        for s in scores:
            m_new = jnp.maximum(m_new, jnp.max(s, axis=-1, keepdims=True))
        corr = jnp.exp(m_old - m_new)
        l = l_ref[...] * corr
        acc = acc_ref[...] * corr
        for s, v in zip(scores, values):
            e = jnp.exp(s - m_new)
            l = l + jnp.sum(e, axis=-1, keepdims=True)
            acc = acc + jnp.dot(e.astype(BF16), v, preferred_element_type=F32)
        m_ref[...] = m_new
        l_ref[...] = l
        acc_ref[...] = acc

    head_bias = jnp.where(same_head, 0.0, -jnp.inf).astype(F32)
    scores = []
    for r in k_refs:
        s = _dot_nt(qall, r[...].astype(BF16))
        scores.append(s + jnp.tile(head_bias, (1, s.shape[1] // LANES)))
    update(scores, [r[...].astype(BF16) for r in v_refs])

    @pl.when(pg == pl.num_programs(1) - 1)
    def _():
        lam = _diff_lambda(lq1_ref, lk1_ref, lq2_ref, lk2_ref, lam_init)
        n_new = kn_ref.shape[0]
        pad = jnp.zeros((LANES - n_new, dqk), F32)
        kn = jnp.concatenate([kn_ref[...], pad], axis=0).astype(BF16)
        vn = jnp.concatenate([vn_ref[...], pad], axis=0).astype(BF16)
        causal = (col // n_heads) <= (row % t)
        s = jnp.where(same_head & causal & (col < n_new), _dot_nt(qall, kn), -jnp.inf)
        update([s], [vn])
        o = acc_ref[...] / l_ref[...]
        for h in range(n_heads):
            r0 = h * 2 * t
            od = o[r0:r0 + t] - lam * o[r0 + t:r0 + 2 * t]
            o_ref[:, h * dv:(h + 1) * dv] = _sub_norm(od, subln_ref[...], lam_init)


def _sample_attention(q, k_new, v_new, cache_k2, cache_v2, page_table, layer_base, lam_params, subln,
                      *, n_heads, head_dim, pp, lam_init):
    db, t, _ = q.shape
    n_pages = page_table.shape[1]
    rows = cache_k2.shape[1]
    dqk = 2 * head_dim
    dv = cache_v2.shape[-1]
    n_rows = n_heads * 2 * t
    small = lambda n: pl.BlockSpec((1, n), lambda b, p, pt: (0, 0))
    per_b = lambda r, n: pl.BlockSpec((None, r, n), lambda b, p, pt: (b, 0, 0))

    def page_spec(i, width):
        return pl.BlockSpec((None, rows, width), lambda b, p, pt: (layer_base + pt[b, p * pp + i], 0, 0))

    est = 2 * pp * rows * (dqk + dv) * 4 + pp * n_rows * rows * 4 * 3
    return pl.pallas_call(
        functools.partial(_sample_attn_kernel, pp=pp, n_heads=n_heads, head_dim=head_dim,
                          lam_init=lam_init),
        grid_spec=pltpu.PrefetchScalarGridSpec(
            num_scalar_prefetch=1,
            grid=(db, n_pages // pp),
            in_specs=[small(head_dim)] * 4 + [small(dv), per_b(t, n_heads * dqk),
                                              per_b(t * n_heads, dqk), per_b(t * n_heads, dv)]
                     + [page_spec(i, dqk) for i in range(pp)] + [page_spec(i, dv) for i in range(pp)],
            out_specs=per_b(t, n_heads * dv),
            scratch_shapes=[pltpu.VMEM((n_rows, dqk), BF16), pltpu.VMEM((n_rows, 1), F32),
                            pltpu.VMEM((n_rows, 1), F32), pltpu.VMEM((n_rows, dv), F32)]),
        out_shape=jax.ShapeDtypeStruct((db, t, n_heads * dv), F32),
        compiler_params=pltpu.CompilerParams(
            dimension_semantics=("parallel", "arbitrary"), vmem_limit_bytes=_vmem_limit(est)),
        name="sample_attn",
    )(page_table, *lam_params, subln, q, k_new, v_new, *([cache_k2] * pp), *([cache_v2] * pp))


def _post_kernel(x_ref, o_ref, ca_ref, g_ref, hf_ref,
                 wa_ref, wb_ref, wout_ref, nmp_ref, nfp_ref, wup_ref, fcw_ref, fcb_ref, wdown_ref, nfq_ref,
                 y_ref, fs_ref, fext_ref, *, nb, t, ffn_chunk):
    ti = pl.program_id(1)
    rows = nb * t
    d = x_ref.shape[-1]
    ffn = fext_ref.shape[-1]

    @pl.when(ti == 0)
    def _():
        fext_ref[:, 0:HIST_F_PAD, :] = hf_ref[...]

    y_b = jnp.dot(ca_ref[...], wb_ref[...], preferred_element_type=F32)
    y_a = jnp.dot(o_ref[...].astype(BF16), wa_ref[...], preferred_element_type=F32)
    mix = g_ref[:, :d].astype(F32) * y_a + g_ref[:, d:].astype(F32) * y_b
    x1 = x_ref[...] + _rms(jnp.dot(mix.astype(BF16), wout_ref[...], preferred_element_type=F32),
                           nmp_ref[...])

    h2 = _rms(x1, nfp_ref[...]).astype(BF16)
    acc = jnp.zeros((rows, d), F32)
    for c0 in range(0, ffn, ffn_chunk):
        cs = slice(c0, c0 + ffn_chunk)
        fg = jnp.dot(h2, wup_ref[:, cs], preferred_element_type=F32)
        fu = jnp.dot(h2, wup_ref[:, ffn + c0:ffn + c0 + ffn_chunk], preferred_element_type=F32)
        fext_ref[:, HIST_F_PAD:HIST_F_PAD + t, cs] = fg.reshape(nb, t, ffn_chunk)
        cv = _causal_dwconv(fext_ref, fcw_ref, fcb_ref[:, cs], t=t, pad=HIST_F_PAD, cols=cs)
        f = jax.nn.gelu(cv).reshape(rows, ffn_chunk) * fu
        acc = acc + jnp.dot(f.astype(BF16), wdown_ref[cs, :], preferred_element_type=F32)
    y_ref[...] = x1 + _rms(acc, nfq_ref[...])

    tail = fext_ref[:, t:t + HIST_F_PAD, :]
    fs_ref[...] = tail
    fext_ref[:, 0:HIST_F_PAD, :] = tail


def _post(x2d, o2d, ca2d, g2d, hist_f, w, *, n_seq, nb, t, ffn_chunk):
    m, d = x2d.shape
    dv = o2d.shape[1]
    d_c = ca2d.shape[1]
    ffn = w["w_down"].shape[0]
    tiles = (m // n_seq) // t
    rows = nb * t
    row2 = lambda n: pl.BlockSpec((rows, n), lambda b, i: (b * tiles + i, 0))
    hist = pl.BlockSpec((nb, HIST_F_PAD, ffn), lambda b, i: (b, 0, 0))
    weights = [w["w_a"], w["w_b"], w["w_out"], w["norm_mix_post"], w["norm_ffn_pre"], w["w_up"],
               w["ffn_conv_w"], w["ffn_conv_b"], w["w_down"], w["norm_ffn_post"]]
    w_bytes = sum(int(a.size) * a.dtype.itemsize for a in weights)
    est = (w_bytes + 2 * rows * (d * 4 * 2 + dv * o2d.dtype.itemsize + d_c * 2 + 2 * d * 2)
           + nb * (HIST_F_PAD + t) * ffn * 4 + 10 * rows * d * 4 + 8 * rows * ffn_chunk * 4)
    return pl.pallas_call(
        functools.partial(_post_kernel, nb=nb, t=t, ffn_chunk=ffn_chunk),
        grid=(n_seq // nb, tiles),
        in_specs=[row2(d), row2(dv), row2(d_c), row2(2 * d), hist] + [_resident(a.shape) for a in weights],
        out_specs=[row2(d), hist],
        out_shape=[jax.ShapeDtypeStruct(x2d.shape, F32), jax.ShapeDtypeStruct((n_seq, HIST_F_PAD, ffn), F32)],
        scratch_shapes=[pltpu.VMEM((nb, HIST_F_PAD + t, ffn), F32)],
        compiler_params=pltpu.CompilerParams(
            dimension_semantics=("parallel", "arbitrary"), vmem_limit_bytes=_vmem_limit(est)),
        name="post",
    )(x2d, o2d, ca2d, g2d, hist_f, *weights)


def _pad_history(hist, rows):
    return jnp.pad(hist, ((0, 0), (rows - hist.shape[1], 0), (0, 0)))


def kernel(x_prompt, x_sample, cache_k, cache_v, state_conv, state_ffn, page_table, norm_mix_pre, w_in, b_glu, b_gate, lambda_q1, lambda_k1, lambda_q2, lambda_k2, subln, w_a, conv_w, conv_b, ln_g, ln_b, w_b, w_out, norm_mix_post, norm_ffn_pre, w_up, ffn_conv_w, ffn_conv_b, w_down, norm_ffn_post):
    depth = w_in.shape[0]
    bp, sp, d = x_prompt.shape
    bs, ts, _ = x_sample.shape
    _, n_pool, page, n_heads, dqk = cache_k.shape
    head_dim = dqk // 2
    dv = cache_v.shape[-1]
    d_qk, d_v = n_heads * dqk, n_heads * dv
    cch = conv_w.shape[-1]
    conv_k = conv_w.shape[1]
    ffn = w_down.shape[1]
    ffn_k = ffn_conv_w.shape[1]
    scale = head_dim ** -0.5

    cache_k2 = cache_k.reshape(depth * n_pool, page * n_heads, dqk)
    cache_v2 = cache_v.reshape(depth * n_pool, page * n_heads, dv)

    yp = x_prompt.reshape(bp * sp, d)
    ys = x_sample.reshape(bs * ts, d)
    zeros_c = jnp.zeros((bp, HIST_C_PAD, cch), F32)
    zeros_f = jnp.zeros((bp, HIST_F_PAD, ffn), F32)
    outs = [[] for _ in range(8)]
    for l in range(depth):
        lam_init = 0.8 - 0.6 * math.exp(-0.3 * l)
        row = lambda a: a[l].reshape(1, -1)
        lam_params = [row(lambda_q1), row(lambda_k1), row(lambda_q2), row(lambda_k2)]
        w_in_bf = w_in[l].astype(BF16)
        w = {
            "norm_mix_pre": row(norm_mix_pre), "w_in": w_in_bf, "b_glu": row(b_glu), "b_gate": row(b_gate),
            "conv_w": conv_w[l], "conv_b": row(conv_b), "ln_g": row(ln_g), "ln_b": row(ln_b),
            "w_a": w_a[l].astype(BF16), "w_b": w_b[l].astype(BF16), "w_out": w_out[l].astype(BF16),
            "norm_mix_post": row(norm_mix_post), "norm_ffn_pre": row(norm_ffn_pre),
            "w_up": w_up[l].astype(BF16), "ffn_conv_w": ffn_conv_w[l], "ffn_conv_b": row(ffn_conv_b),
            "w_down": w_down[l].astype(BF16), "norm_ffn_post": row(norm_ffn_post),
        }
        proj_kw = dict(d_qk=d_qk, d_v=d_v, scale=scale)

        q, k, v, g, ca, cs = _proj_in(yp, zeros_c, w, n_seq=bp, nb=1, t=512, q_dtype=BF16, **proj_kw)
        o = _prompt_attention(q.reshape(bp, sp, d_qk), k.reshape(bp, sp, d_qk), v.reshape(bp, sp, d_v),
                              lam_params, row(subln), n_heads=n_heads, head_dim=head_dim, tq=256, heads=2,
                              lam_init=lam_init)
        yp, fs = _post(yp, o.reshape(bp * sp, d_v), ca, g, zeros_f, w, n_seq=bp, nb=1, t=256, ffn_chunk=256)
        outs[0].append(k.reshape(bp, sp, n_heads, dqk))
        outs[1].append(v.reshape(bp, sp, n_heads, dv))
        outs[2].append(cs[:, HIST_C_PAD - (conv_k - 1):])
        outs[3].append(fs[:, HIST_F_PAD - (ffn_k - 1):])

        q, k, v, g, ca, cs = _proj_in(ys, _pad_history(state_conv[l], HIST_C_PAD), w, n_seq=bs, nb=bs, t=ts,
                                      q_dtype=F32, **proj_kw)
        k_rows = k.reshape(bs, ts, n_heads, dqk)
        v_rows = v.reshape(bs, ts, n_heads, dv)
        o = _sample_attention(q.reshape(bs, ts, d_qk), k_rows.reshape(bs, ts * n_heads, dqk),
                              v_rows.reshape(bs, ts * n_heads, dv), cache_k2, cache_v2, page_table,
                              l * n_pool, lam_params, row(subln), n_heads=n_heads, head_dim=head_dim,
                              pp=8, lam_init=lam_init)
        ys, fs = _post(ys, o.reshape(bs * ts, d_v), ca, g, _pad_history(state_ffn[l], HIST_F_PAD), w,
                       n_seq=bs, nb=bs, t=ts, ffn_chunk=256)
        outs[4].append(k_rows)
        outs[5].append(v_rows)
        outs[6].append(cs[:, HIST_C_PAD - (conv_k - 1):])
        outs[7].append(fs[:, HIST_F_PAD - (ffn_k - 1):])

    return (yp.reshape(bp, sp, d), ys.reshape(bs, ts, d)) + tuple(jnp.stack(o) for o in outs)
```

```python
import functools
import math

import jax
import jax.numpy as jnp
from jax import lax
from jax.experimental import pallas as pl
from jax.experimental.pallas import tpu as pltpu

F32 = jnp.float32
BF16 = jnp.bfloat16

V7X_VMEM_BYTES = 64 * 1024 * 1024
LANES = 128
SUBLANES = 8

EPS = 1e-6
LN_EPS = 1e-5
NEG_BIG = -1e30

HIST_C_PAD = 32
HIST_F_PAD = 8
CONV_ROW_CHUNK = 32
ONES_ROWS = 16
LOG2_E = math.log2(math.e)
PAGE_GROUP = 4


def _vmem_limit(estimate_bytes):
    return int(min(V7X_VMEM_BYTES - 4 * 1024 * 1024, max(estimate_bytes, 16 * 1024 * 1024)))


def _resident(shape):
    nd = len(shape)
    return pl.BlockSpec(shape, lambda *_: (0,) * nd, pipeline_mode=pl.Buffered(1))


def _rms(x, gain):
    return x * lax.rsqrt(jnp.mean(x * x, axis=-1, keepdims=True) + EPS) * gain


def _diff_lambda(lq1_ref, lk1_ref, lq2_ref, lk2_ref, lam_init):
    a = jnp.sum(lq1_ref[...] * lk1_ref[...], axis=-1, keepdims=True)
    b = jnp.sum(lq2_ref[...] * lk2_ref[...], axis=-1, keepdims=True)
    return jnp.exp(a) - jnp.exp(b) + lam_init


def _dot_nt(a, b):
    return lax.dot_general(a, b, (((1,), (1,)), ((), ())), preferred_element_type=F32)


def _split_halves(qh, head_dim):
    lane = lax.broadcasted_iota(jnp.int32, qh.shape, 1)
    zero = jnp.zeros_like(qh)
    return jnp.concatenate([jnp.where(lane < head_dim, qh, zero),
                            jnp.where(lane >= head_dim, qh, zero)], axis=0)


def _sub_norm(o, subln, lam_init):
    return _rms(o, subln) * (1.0 - lam_init)


def _causal_dwconv(ext, w_ref, bias, *, t, pad, row0=0, wcols=slice(None), row_chunk=None):
    k = w_ref.shape[0]
    base = pad - (k - 1)
    row_chunk = row_chunk or t
    outs = []
    for c0 in range(row0, row0 + t, row_chunk):
        n = min(row_chunk, row0 + t - c0)
        y = None
        for r in range(SUBLANES):
            taps = [j for j in range(k) if (base + j) % SUBLANES == r]
            if not taps:
                continue
            span = n if r == 0 else n + SUBLANES
            u = None
            for j in taps:
                a0 = c0 + base + j - r
                term = ext[:, a0:a0 + span, :] * w_ref[pl.ds(j, 1), wcols]
                u = term if u is None else u + term
            u = u if r == 0 else u[:, r:r + n, :]
            y = u if y is None else y + u
        outs.append(y + bias)
    return outs[0] if len(outs) == 1 else jnp.concatenate(outs, axis=1)


def _proj_in_kernel(x_ref, gain_ref, w_ref, bglu_ref, bgate_ref, hc_ref, cw_ref, cb_ref, lng_ref, lnb_ref,
                    q_ref, k_ref, v_ref, g_ref, ca_ref, cs_ref, cext_ref, *, nb, t, d_qk, d_v, scale):
    ti = pl.program_id(1)
    d_c = ca_ref.shape[-1]
    d_g = g_ref.shape[-1]
    h = _rms(x_ref[...], gain_ref[...]).astype(BF16)

    def proj(c0, n):
        return jnp.dot(h, w_ref[:, c0:c0 + n], preferred_element_type=F32)

    @pl.when(ti == 0)
    def _():
        cext_ref[:, 0:HIST_C_PAD, :] = hc_ref[...]

    c0 = 2 * d_qk + d_v
    glu_a = proj(c0, d_c) + bglu_ref[:, :d_c]
    glu_b = proj(c0 + d_c, d_c) + bglu_ref[:, d_c:]
    cext_ref[:, HIST_C_PAD:HIST_C_PAD + t, :] = (glu_a * jax.nn.sigmoid(glu_b)).reshape(nb, t, d_c)

    q_ref[...] = (proj(0, d_qk) * scale).astype(q_ref.dtype)
    k_ref[...] = proj(d_qk, d_qk)
    v_ref[...] = proj(2 * d_qk, d_v)
    half = d_g // 2
    for i in range(2):
        gates = proj(c0 + 2 * d_c + i * half, half) + bgate_ref[:, i * half:(i + 1) * half]
        g_ref[:, i * half:(i + 1) * half] = jax.nn.sigmoid(gates).astype(g_ref.dtype)

    conv = _causal_dwconv(cext_ref, cw_ref, cb_ref[...], t=t, pad=HIST_C_PAD,
                          row_chunk=min(t, CONV_ROW_CHUNK))
    mu = jnp.mean(conv, axis=-1, keepdims=True)
    xc = conv - mu
    var = jnp.mean(xc * xc, axis=-1, keepdims=True)
    ln = xc * lax.rsqrt(var + LN_EPS) * lng_ref[...] + lnb_ref[...]
    ca_ref[...] = jax.nn.silu(ln).reshape(nb * t, d_c).astype(ca_ref.dtype)

    tail = cext_ref[:, t:t + HIST_C_PAD, :]
    cs_ref[...] = tail
    cext_ref[:, 0:HIST_C_PAD, :] = tail


def _proj_in(x2d, hist_c, w, *, n_seq, nb, t, q_dtype, d_qk, d_v, scale):
    m, d = x2d.shape
    tiles = (m // n_seq) // t
    rows = nb * t
    n_in = w["w_in"].shape[1]
    d_c = w["conv_w"].shape[1]
    d_g = n_in - 2 * d_qk - d_v - 2 * d_c
    row = lambda n: pl.BlockSpec((rows, n), lambda b, i: (b * tiles + i, 0))
    hist = pl.BlockSpec((nb, HIST_C_PAD, d_c), lambda b, i: (b, 0, 0))
    weights = [w["norm_mix_pre"], w["w_in"], w["b_glu"], w["b_gate"]]
    conv_weights = [w["conv_w"], w["conv_b"], w["ln_g"], w["ln_b"]]
    est = (d * n_in * 2 + 2 * rows * (d * 4 + d_qk * 8 + d_v * 4 + d_c * 2 + d_g * 2)
           + nb * (HIST_C_PAD + t) * d_c * 4 + 8 * rows * max(d_qk, d_g // 2) * 4)
    return pl.pallas_call(
        functools.partial(_proj_in_kernel, nb=nb, t=t, d_qk=d_qk, d_v=d_v, scale=scale),
        grid=(n_seq // nb, tiles),
        in_specs=([row(d)] + [_resident(a.shape) for a in weights] + [hist]
                  + [_resident(a.shape) for a in conv_weights]),
        out_specs=[row(d_qk), row(d_qk), row(d_v), row(d_g), row(d_c), hist],
        out_shape=[jax.ShapeDtypeStruct((m, d_qk), q_dtype), jax.ShapeDtypeStruct((m, d_qk), F32),
                   jax.ShapeDtypeStruct((m, d_v), F32), jax.ShapeDtypeStruct((m, d_g), BF16),
                   jax.ShapeDtypeStruct((m, d_c), BF16),
                   jax.ShapeDtypeStruct((n_seq, HIST_C_PAD, d_c), F32)],
        scratch_shapes=[pltpu.VMEM((nb, HIST_C_PAD + t, d_c), F32)],
        compiler_params=pltpu.CompilerParams(
            dimension_semantics=("parallel", "arbitrary"), vmem_limit_bytes=_vmem_limit(est)),
        name="proj_in",
    )(x2d, *weights, hist_c, *conv_weights)


def _prompt_attn_kernel(lq1_ref, lk1_ref, lq2_ref, lk2_ref, subln_ref, q_ref, k_ref, v_ref, o_ref, vt_ref,
                        *, tq, head_dim, heads, lam_init):
    qi = pl.program_id(2)
    dqk = 2 * head_dim
    dv = vt_ref.shape[1] - ONES_ROWS
    n_tiles = k_ref.shape[0] // tq

    @pl.when(qi == 0)
    def _():
        eye = (lax.broadcasted_iota(jnp.int32, (dv, dv), 0)
               == lax.broadcasted_iota(jnp.int32, (dv, dv), 1)).astype(BF16)
        for hh in range(heads):
            vt_ref[hh, 0:dv, :] = _dot_nt(eye, v_ref[:, hh * dv:(hh + 1) * dv].astype(BF16)).astype(BF16)
            vt_ref[hh, dv:dv + ONES_ROWS, :] = jnp.ones((ONES_ROWS, vt_ref.shape[2]), BF16)

    lam = _diff_lambda(lq1_ref, lk1_ref, lq2_ref, lk2_ref, lam_init)

    def score(hh, n):
        qs = _split_halves(q_ref[:, hh * dqk:(hh + 1) * dqk], head_dim)
        scores, m = [], None
        for j in range(n):
            kb = k_ref[j * tq:(j + 1) * tq, hh * dqk:(hh + 1) * dqk].astype(BF16)
            s = _dot_nt(kb, qs)
            if j == n - 1:
                key = lax.broadcasted_iota(jnp.int32, s.shape, 0)
                qry = lax.broadcasted_iota(jnp.int32, s.shape, 1)
                qry = jnp.where(qry >= tq, qry - tq, qry)
                s = jnp.where(key <= qry, s, -jnp.inf)
            mj = jnp.max(s, axis=0, keepdims=True)
            m = mj if m is None else jnp.maximum(m, mj)
            scores.append(s)
        return scores, m

    def finish(hh, scores, m):
        acc = jnp.zeros((dv + ONES_ROWS, 2 * tq), F32)
        for j, s in enumerate(scores):
            e = jnp.exp2((s - m).astype(BF16))
            acc = acc + jnp.dot(vt_ref[hh, :, j * tq:(j + 1) * tq], e, preferred_element_type=F32)
        o = acc[:dv] / acc[dv:dv + 1]
        od = (o[:, :tq] - lam * o[:, tq:]).T
        o_ref[:, hh * dv:(hh + 1) * dv] = _sub_norm(od, subln_ref[...], lam_init).astype(o_ref.dtype)

    for n in range(1, n_tiles + 1):
        @pl.when(qi == n - 1)
        def _(n=n):
            pending = score(0, n)
            for hh in range(1, heads):
                upcoming = score(hh, n)
                finish(hh - 1, *pending)
                pending = upcoming
            finish(heads - 1, *pending)


def _prompt_attention(q, k, v, lam_params, subln, *, n_heads, head_dim, tq, heads, lam_init):
    b, s, _ = q.shape
    dv = v.shape[-1] // n_heads
    dqk = 2 * head_dim
    small = lambda n: pl.BlockSpec((1, n), lambda bi, h, i: (0, 0))
    est = (2 * heads * (s * (dqk + dv) * 4 + tq * (dqk + dv) * 2) + heads * dv * s * 2
           + heads * s * 2 * tq * 4 * 2)
    return pl.pallas_call(
        functools.partial(_prompt_attn_kernel, tq=tq, head_dim=head_dim, heads=heads, lam_init=lam_init),
        grid=(b, n_heads // heads, s // tq),
        in_specs=[small(head_dim)] * 4 + [small(dv),
                  pl.BlockSpec((None, tq, heads * dqk), lambda bi, h, i: (bi, i, h)),
                  pl.BlockSpec((None, s, heads * dqk), lambda bi, h, i: (bi, 0, h)),
                  pl.BlockSpec((None, s, heads * dv), lambda bi, h, i: (bi, 0, h))],
        out_specs=pl.BlockSpec((None, tq, heads * dv), lambda bi, h, i: (bi, i, h)),
        out_shape=jax.ShapeDtypeStruct((b, s, n_heads * dv), BF16),
        scratch_shapes=[pltpu.VMEM((heads, dv + ONES_ROWS, s), BF16)],
        compiler_params=pltpu.CompilerParams(
            dimension_semantics=("parallel", "parallel", "arbitrary"),
            vmem_limit_bytes=_vmem_limit(est)),
        name="prompt_attn",
    )(*lam_params, subln, q, k, v)


def _sample_attn_kernel(pt_ref, lq1_ref, lk1_ref, lq2_ref, lk2_ref, subln_ref, q_ref, kn_ref, vn_ref,
                        *rest, pp, n_heads, head_dim, lam_init):
    del pt_ref
    k_refs, v_refs = rest[:pp], rest[pp:2 * pp]
    o_ref, qall_ref, vext_ref, m_ref, acc_ref = rest[2 * pp:]
    pg = pl.program_id(1)
    t = q_ref.shape[0]
    dqk = 2 * head_dim
    dv = vn_ref.shape[-1]
    n_rows = n_heads * 2 * t

    @pl.when(pg == 0)
    def _():
        qall_ref[...] = jnp.concatenate(
            [_split_halves(q_ref[:, h * dqk:(h + 1) * dqk], head_dim) for h in range(n_heads)],
            axis=0).astype(BF16)
        vext_ref[:, :, dv:] = jnp.ones((pp, vext_ref.shape[1], vext_ref.shape[2] - dv), BF16)
        m_ref[...] = jnp.full(m_ref.shape, NEG_BIG, F32)
        acc_ref[...] = jnp.zeros(acc_ref.shape, F32)

    qall = qall_ref[...]
    row = lax.broadcasted_iota(jnp.int32, (n_rows, LANES), 0)
    col = lax.broadcasted_iota(jnp.int32, (n_rows, LANES), 1)
    same_head = (col % n_heads) == (row // (2 * t))

    def fold_max(s):
        blocks = [s[:, c:c + LANES] for c in range(0, s.shape[1], LANES)]
        return functools.reduce(jnp.maximum, blocks)

    def update(scores, valid, values):
        m_old = m_ref[...]
        folded = functools.reduce(jnp.maximum, [fold_max(s) for s in scores])
        m_new = jnp.maximum(m_old, jnp.max(jnp.where(valid, folded, -jnp.inf), axis=-1, keepdims=True))
        shift = jnp.where(valid, m_new, jnp.inf)
        acc = acc_ref[...] * jnp.exp2(m_old - m_new)
        for s, v in zip(scores, values):
            e = jnp.exp2((s - jnp.tile(shift, (1, s.shape[1] // LANES))).astype(BF16))
            acc = acc + jnp.dot(e, v, preferred_element_type=F32)
        m_ref[...] = m_new
        acc_ref[...] = acc

    for i, r in enumerate(v_refs):
        vext_ref[i, :, 0:dv] = r[...].astype(BF16)
    scores = [_dot_nt(qall, r[...].astype(BF16)) for r in k_refs]
    for g in range(0, pp, PAGE_GROUP):
        update(scores[g:g + PAGE_GROUP], same_head, [vext_ref[i] for i in range(g, min(g + PAGE_GROUP, pp))])

    @pl.when(pg == pl.num_programs(1) - 1)
    def _():
        lam = _diff_lambda(lq1_ref, lk1_ref, lq2_ref, lk2_ref, lam_init)
        n_new = kn_ref.shape[0]
        pad = jnp.zeros((LANES - n_new, dqk), F32)
        kn = jnp.concatenate([kn_ref[...], pad], axis=0).astype(BF16)
        vn = jnp.concatenate([vn_ref[...], pad], axis=0).astype(BF16)
        vn = jnp.concatenate([vn, jnp.ones((LANES, acc_ref.shape[1] - dv), BF16)], axis=1)
        causal = (col // n_heads) <= (row % t)
        update([_dot_nt(qall, kn)], same_head & causal & (col < n_new), [vn])
        o = acc_ref[:, 0:dv] / acc_ref[:, dv:2 * dv]
        for h in range(n_heads):
            r0 = h * 2 * t
            od = o[r0:r0 + t] - lam * o[r0 + t:r0 + 2 * t]
            o_ref[:, h * dv:(h + 1) * dv] = _sub_norm(od, subln_ref[...], lam_init)


def _sample_attention(q, k_new, v_new, cache_k2, cache_v2, page_table, layer_base, lam_params, subln,
                      *, n_heads, head_dim, pp, lam_init):
    db, t, _ = q.shape
    n_pages = page_table.shape[1]
    rows = cache_k2.shape[1]
    dqk = 2 * head_dim
    dv = cache_v2.shape[-1]
    n_rows = n_heads * 2 * t
    small = lambda n: pl.BlockSpec((1, n), lambda b, p, pt: (0, 0))
    per_b = lambda r, n: pl.BlockSpec((None, r, n), lambda b, p, pt: (b, 0, 0))

    def page_spec(i, width):
        return pl.BlockSpec((None, rows, width), lambda b, p, pt: (layer_base + pt[b, p * pp + i], 0, 0))

    est = 2 * pp * rows * (dqk + dv) * 4 + pp * rows * 2 * dv * 2 + pp * n_rows * rows * 4 * 2
    return pl.pallas_call(
        functools.partial(_sample_attn_kernel, pp=pp, n_heads=n_heads, head_dim=head_dim,
                          lam_init=lam_init),
        grid_spec=pltpu.PrefetchScalarGridSpec(
            num_scalar_prefetch=1,
            grid=(db, n_pages // pp),
            in_specs=[small(head_dim)] * 4 + [small(dv), per_b(t, n_heads * dqk),
                                              per_b(t * n_heads, dqk), per_b(t * n_heads, dv)]
                     + [page_spec(i, dqk) for i in range(pp)] + [page_spec(i, dv) for i in range(pp)],
            out_specs=per_b(t, n_heads * dv),
            scratch_shapes=[pltpu.VMEM((n_rows, dqk), BF16), pltpu.VMEM((pp, rows, 2 * dv), BF16),
                            pltpu.VMEM((n_rows, 1), F32), pltpu.VMEM((n_rows, 2 * dv), F32)]),
        out_shape=jax.ShapeDtypeStruct((db, t, n_heads * dv), F32),
        compiler_params=pltpu.CompilerParams(
            dimension_semantics=("parallel", "arbitrary"), vmem_limit_bytes=_vmem_limit(est)),
        name="sample_attn",
    )(page_table, *lam_params, subln, q, k_new, v_new, *([cache_k2] * pp), *([cache_v2] * pp))


def _post_kernel(x_ref, o_ref, ca_ref, g_ref, hf_ref,
                 wa_ref, wb_ref, wout_ref, nmp_ref, nfp_ref, wup_ref, fcw_ref, fcb_ref, wdown_ref, nfq_ref,
                 y_ref, fs_ref, fhist_ref, *, nb, t):
    ti = pl.program_id(1)
    rows = nb * t
    d = x_ref.shape[-1]
    ffn = fhist_ref.shape[-1]

    @pl.when(ti == 0)
    def _():
        fhist_ref[...] = hf_ref[...]

    y_b = jnp.dot(ca_ref[...], wb_ref[...], preferred_element_type=F32)
    y_a = jnp.dot(o_ref[...].astype(BF16), wa_ref[...], preferred_element_type=F32)
    mix = g_ref[:, :d].astype(F32) * y_a + g_ref[:, d:].astype(F32) * y_b
    x1 = x_ref[...] + _rms(jnp.dot(mix.astype(BF16), wout_ref[...], preferred_element_type=F32),
                           nmp_ref[...])

    h2 = _rms(x1, nfp_ref[...]).astype(BF16)
    fg = jnp.dot(h2, wup_ref[:, :ffn], preferred_element_type=F32)
    fu = jnp.dot(h2, wup_ref[:, ffn:], preferred_element_type=F32)
    fext = jnp.concatenate([fhist_ref[...], fg.reshape(nb, t, ffn)], axis=1)
    cv = _causal_dwconv(fext, fcw_ref, fcb_ref[...], t=t, pad=HIST_F_PAD)
    f = jax.nn.gelu(cv).reshape(rows, ffn) * fu
    down = jnp.dot(f.astype(BF16), wdown_ref[...], preferred_element_type=F32)
    y_ref[...] = x1 + _rms(down, nfq_ref[...])

    tail = fext[:, t:t + HIST_F_PAD, :]
    fs_ref[...] = tail
    fhist_ref[...] = tail


def _post(x2d, o2d, ca2d, g2d, hist_f, w, *, n_seq, nb, t):
    m, d = x2d.shape
    dv = o2d.shape[1]
    d_c = ca2d.shape[1]
    ffn = w["w_down"].shape[0]
    tiles = (m // n_seq) // t
    rows = nb * t
    row2 = lambda n: pl.BlockSpec((rows, n), lambda b, i: (b * tiles + i, 0))
    hist = pl.BlockSpec((nb, HIST_F_PAD, ffn), lambda b, i: (b, 0, 0))
    weights = [w["w_a"], w["w_b"], w["w_out"], w["norm_mix_post"], w["norm_ffn_pre"], w["w_up"],
               w["ffn_conv_w"], w["ffn_conv_b"], w["w_down"], w["norm_ffn_post"]]
    w_bytes = sum(int(a.size) * a.dtype.itemsize for a in weights)
    est = (w_bytes + 2 * rows * (d * 4 * 2 + dv * o2d.dtype.itemsize + d_c * 2 + 2 * d * 2)
           + nb * HIST_F_PAD * ffn * 4 + 10 * rows * d * 4 + 6 * rows * ffn * 4)
    return pl.pallas_call(
        functools.partial(_post_kernel, nb=nb, t=t),
        grid=(n_seq // nb, tiles),
        in_specs=[row2(d), row2(dv), row2(d_c), row2(2 * d), hist] + [_resident(a.shape) for a in weights],
        out_specs=[row2(d), hist],
        out_shape=[jax.ShapeDtypeStruct(x2d.shape, F32), jax.ShapeDtypeStruct((n_seq, HIST_F_PAD, ffn), F32)],
        scratch_shapes=[pltpu.VMEM((nb, HIST_F_PAD, ffn), F32)],
        compiler_params=pltpu.CompilerParams(
            dimension_semantics=("parallel", "arbitrary"), vmem_limit_bytes=_vmem_limit(est)),
        name="post",
    )(x2d, o2d, ca2d, g2d, hist_f, *weights)


def _pad_history(hist, rows):
    return jnp.pad(hist, ((0, 0), (rows - hist.shape[1], 0), (0, 0)))


def kernel(x_prompt, x_sample, cache_k, cache_v, state_conv, state_ffn, page_table, norm_mix_pre, w_in, b_glu, b_gate, lambda_q1, lambda_k1, lambda_q2, lambda_k2, subln, w_a, conv_w, conv_b, ln_g, ln_b, w_b, w_out, norm_mix_post, norm_ffn_pre, w_up, ffn_conv_w, ffn_conv_b, w_down, norm_ffn_post):
    depth = w_in.shape[0]
    bp, sp, d = x_prompt.shape
    bs, ts, _ = x_sample.shape
    _, n_pool, page, n_heads, dqk = cache_k.shape
    head_dim = dqk // 2
    dv = cache_v.shape[-1]
    d_qk, d_v = n_heads * dqk, n_heads * dv
    cch = conv_w.shape[-1]
    conv_k = conv_w.shape[1]
    ffn = w_down.shape[1]
    ffn_k = ffn_conv_w.shape[1]
    scale = head_dim ** -0.5 * LOG2_E

    cache_k2 = cache_k.reshape(depth * n_pool, page * n_heads, dqk)
    cache_v2 = cache_v.reshape(depth * n_pool, page * n_heads, dv)

    yp = x_prompt.reshape(bp * sp, d)
    ys = x_sample.reshape(bs * ts, d)
    zeros_c = jnp.zeros((bp, HIST_C_PAD, cch), F32)
    zeros_f = jnp.zeros((bp, HIST_F_PAD, ffn), F32)
    outs = [[] for _ in range(8)]
    for l in range(depth):
        lam_init = 0.8 - 0.6 * math.exp(-0.3 * l)
        row = lambda a: a[l].reshape(1, -1)
        lam_params = [row(lambda_q1), row(lambda_k1), row(lambda_q2), row(lambda_k2)]
        w_in_bf = w_in[l].astype(BF16)
        w = {
            "norm_mix_pre": row(norm_mix_pre), "w_in": w_in_bf, "b_glu": row(b_glu), "b_gate": row(b_gate),
            "conv_w": conv_w[l], "conv_b": row(conv_b), "ln_g": row(ln_g), "ln_b": row(ln_b),
            "w_a": w_a[l].astype(BF16), "w_b": w_b[l].astype(BF16), "w_out": w_out[l].astype(BF16),
            "norm_mix_post": row(norm_mix_post), "norm_ffn_pre": row(norm_ffn_pre),
            "w_up": w_up[l].astype(BF16), "ffn_conv_w": ffn_conv_w[l], "ffn_conv_b": row(ffn_conv_b),
            "w_down": w_down[l].astype(BF16), "norm_ffn_post": row(norm_ffn_post),
        }
        proj_kw = dict(d_qk=d_qk, d_v=d_v, scale=scale)

        q, k, v, g, ca, cs = _proj_in(yp, zeros_c, w, n_seq=bp, nb=1, t=512, q_dtype=BF16, **proj_kw)
        o = _prompt_attention(q.reshape(bp, sp, d_qk), k.reshape(bp, sp, d_qk), v.reshape(bp, sp, d_v),
                              lam_params, row(subln), n_heads=n_heads, head_dim=head_dim, tq=256, heads=4,
                              lam_init=lam_init)
        yp, fs = _post(yp, o.reshape(bp * sp, d_v), ca, g, zeros_f, w, n_seq=bp, nb=1, t=256)
        outs[0].append(k.reshape(bp, sp, n_heads, dqk))
        outs[1].append(v.reshape(bp, sp, n_heads, dv))
        outs[2].append(cs[:, HIST_C_PAD - (conv_k - 1):])
        outs[3].append(fs[:, HIST_F_PAD - (ffn_k - 1):])

        q, k, v, g, ca, cs = _proj_in(ys, _pad_history(state_conv[l], HIST_C_PAD), w, n_seq=bs, nb=bs, t=ts,
                                      q_dtype=F32, **proj_kw)
        k_rows = k.reshape(bs, ts, n_heads, dqk)
        v_rows = v.reshape(bs, ts, n_heads, dv)
        o = _sample_attention(q.reshape(bs, ts, d_qk), k_rows.reshape(bs, ts * n_heads, dqk),
                              v_rows.reshape(bs, ts * n_heads, dv), cache_k2, cache_v2, page_table,
                              l * n_pool, lam_params, row(subln), n_heads=n_heads, head_dim=head_dim,
                              pp=16, lam_init=lam_init)
        ys, fs = _post(ys, o.reshape(bs * ts, d_v), ca, g, _pad_history(state_ffn[l], HIST_F_PAD), w,
                       n_seq=bs, nb=bs, t=ts)
        outs[4].append(k_rows)
        outs[5].append(v_rows)
        outs[6].append(cs[:, HIST_C_PAD - (conv_k - 1):])
        outs[7].append(fs[:, HIST_F_PAD - (ffn_k - 1):])

    return (yp.reshape(bp, sp, d), ys.reshape(bs, ts, d)) + tuple(jnp.stack(o) for o in outs)
```

```python
import functools
import math

import jax
import jax.numpy as jnp
from jax import lax
from jax.experimental import pallas as pl
from jax.experimental.pallas import tpu as pltpu

F32 = jnp.float32
BF16 = jnp.bfloat16

V7X_VMEM_BYTES = 64 * 1024 * 1024
LANES = 128
SUBLANES = 8

EPS = 1e-6
LN_EPS = 1e-5
NEG_BIG = -1e30

HIST_C_PAD = 32
HIST_F_PAD = 8
POST_ROW_PARTS = 2
LOG2_E = math.log2(math.e)
PAGE_GROUP = 4

def _vmem_limit(estimate_bytes):
    return int(min(V7X_VMEM_BYTES - 4 * 1024 * 1024, max(estimate_bytes, 16 * 1024 * 1024)))


def _resident(shape):
    nd = len(shape)
    return pl.BlockSpec(shape, lambda *_: (0,) * nd, pipeline_mode=pl.Buffered(1))


def _rms(x, gain):
    return x * lax.rsqrt(jnp.mean(x * x, axis=-1, keepdims=True) + EPS) * gain


def _diff_lambda(lq1_ref, lk1_ref, lq2_ref, lk2_ref, lam_init):
    a = jnp.sum(lq1_ref[...] * lk1_ref[...], axis=-1, keepdims=True)
    b = jnp.sum(lq2_ref[...] * lk2_ref[...], axis=-1, keepdims=True)
    return jnp.exp(a) - jnp.exp(b) + lam_init


def _dot_nt(a, b):
    return lax.dot_general(a, b, (((1,), (1,)), ((), ())), preferred_element_type=F32)


def _split_halves(qh, head_dim):
    lane = lax.broadcasted_iota(jnp.int32, qh.shape, 1)
    zero = jnp.zeros_like(qh)
    return jnp.concatenate([jnp.where(lane < head_dim, qh, zero),
                            jnp.where(lane >= head_dim, qh, zero)], axis=0)


def _sub_norm(o, subln, lam_init):
    return _rms(o, subln) * (1.0 - lam_init)


def _causal_dwconv(ext, w_ref, bias, *, t, pad):
    k = w_ref.shape[0]
    base = pad - (k - 1)
    y = None
    for r in range(SUBLANES):
        taps = [j for j in range(k) if (base + j) % SUBLANES == r]
        if not taps:
            continue
        span = t if r == 0 else t + SUBLANES
        u = None
        for j in taps:
            a0 = base + j - r
            term = ext[:, a0:a0 + span, :] * w_ref[pl.ds(j, 1), :]
            u = term if u is None else u + term
        u = u if r == 0 else u[:, r:r + t, :]
        y = u if y is None else y + u
    return y + bias


def _proj_in_kernel(x_ref, gain_ref, w_ref, bglu_ref, bgate_ref, hc_ref, cw_ref, cb_ref, lng_ref, lnb_ref,
                    q_ref, k_ref, v_ref, g_ref, ca_ref, cs_ref, cext_ref, *, nb, t, d_qk, d_v, scale):
    ti = pl.program_id(1)
    d_c = ca_ref.shape[-1]
    d_g = g_ref.shape[-1]
    h = _rms(x_ref[...], gain_ref[...]).astype(BF16)

    def proj(c0, n):
        return jnp.dot(h, w_ref[:, c0:c0 + n], preferred_element_type=F32)

    @pl.when(ti == 0)
    def _():
        cext_ref[:, 0:HIST_C_PAD, :] = hc_ref[...]

    c0 = 2 * d_qk + d_v
    glu_a = proj(c0, d_c) + bglu_ref[:, :d_c]
    glu_b = proj(c0 + d_c, d_c) + bglu_ref[:, d_c:]
    cext_ref[:, HIST_C_PAD:HIST_C_PAD + t, :] = (glu_a * jax.nn.sigmoid(glu_b)).reshape(nb, t, d_c)

    q_ref[...] = (proj(0, d_qk) * scale).astype(q_ref.dtype)
    k_ref[...] = proj(d_qk, d_qk)
    v_ref[...] = proj(2 * d_qk, d_v)
    half = d_g // 2
    for i in range(2):
        gates = proj(c0 + 2 * d_c + i * half, half) + bgate_ref[:, i * half:(i + 1) * half]
        g_ref[:, i * half:(i + 1) * half] = jax.nn.sigmoid(gates).astype(g_ref.dtype)

    conv = _causal_dwconv(cext_ref, cw_ref, cb_ref[...], t=t, pad=HIST_C_PAD)
    mu = jnp.mean(conv, axis=-1, keepdims=True)
    xc = conv - mu
    var = jnp.mean(xc * xc, axis=-1, keepdims=True)
    ln = xc * lax.rsqrt(var + LN_EPS) * lng_ref[...] + lnb_ref[...]
    ca_ref[...] = jax.nn.silu(ln).reshape(nb * t, d_c).astype(ca_ref.dtype)

    tail = cext_ref[:, t:t + HIST_C_PAD, :]
    cs_ref[...] = tail
    cext_ref[:, 0:HIST_C_PAD, :] = tail


def _proj_in(x2d, hist_c, w, *, n_seq, nb, t, q_dtype, d_qk, d_v, scale):
    m, d = x2d.shape
    tiles = (m // n_seq) // t
    rows = nb * t
    n_in = w["w_in"].shape[1]
    d_c = w["conv_w"].shape[1]
    d_g = n_in - 2 * d_qk - d_v - 2 * d_c
    row = lambda n: pl.BlockSpec((rows, n), lambda b, i: (b * tiles + i, 0))
    hist = pl.BlockSpec((nb, HIST_C_PAD, d_c), lambda b, i: (b, 0, 0))
    weights = [w["norm_mix_pre"], w["w_in"], w["b_glu"], w["b_gate"]]
    conv_weights = [w["conv_w"], w["conv_b"], w["ln_g"], w["ln_b"]]
    est = (d * n_in * 2 + 2 * rows * (d * 4 + d_qk * 8 + d_v * 4 + d_c * 2 + d_g * 2)
           + nb * (HIST_C_PAD + t) * d_c * 4 + 8 * rows * max(d_qk, d_g // 2) * 4)
    return pl.pallas_call(
        functools.partial(_proj_in_kernel, nb=nb, t=t, d_qk=d_qk, d_v=d_v, scale=scale),
        grid=(n_seq // nb, tiles),
        in_specs=([row(d)] + [_resident(a.shape) for a in weights] + [hist]
                  + [_resident(a.shape) for a in conv_weights]),
        out_specs=[row(d_qk), row(d_qk), row(d_v), row(d_g), row(d_c), hist],
        out_shape=[jax.ShapeDtypeStruct((m, d_qk), q_dtype), jax.ShapeDtypeStruct((m, d_qk), F32),
                   jax.ShapeDtypeStruct((m, d_v), F32), jax.ShapeDtypeStruct((m, d_g), BF16),
                   jax.ShapeDtypeStruct((m, d_c), BF16),
                   jax.ShapeDtypeStruct((n_seq, HIST_C_PAD, d_c), F32)],
        scratch_shapes=[pltpu.VMEM((nb, HIST_C_PAD + t, d_c), F32)],
        compiler_params=pltpu.CompilerParams(
            dimension_semantics=("parallel", "arbitrary"), vmem_limit_bytes=_vmem_limit(est)),
        name="proj_in",
    )(x2d, *weights, hist_c, *conv_weights)


def _prompt_attn_kernel(lq1_ref, lk1_ref, lq2_ref, lk2_ref, subln_ref, q_ref, k_ref, v_ref, o_ref, vt_ref,
                        *, tq, head_dim, heads, lam_init):
    qi = pl.program_id(2)
    dqk = 2 * head_dim
    dv = vt_ref.shape[1]
    n_tiles = k_ref.shape[0] // tq

    @pl.when(qi == 0)
    def _():
        eye = (lax.broadcasted_iota(jnp.int32, (dv, dv), 0)
               == lax.broadcasted_iota(jnp.int32, (dv, dv), 1)).astype(BF16)
        for hh in range(heads):
            vt_ref[hh] = _dot_nt(eye, v_ref[:, hh * dv:(hh + 1) * dv].astype(BF16)).astype(BF16)

    lam = _diff_lambda(lq1_ref, lk1_ref, lq2_ref, lk2_ref, lam_init)

    def score(hh, n):
        qs = _split_halves(q_ref[:, hh * dqk:(hh + 1) * dqk], head_dim)
        kb = k_ref[0:n * tq, hh * dqk:(hh + 1) * dqk].astype(BF16)
        s_all = _dot_nt(kb, qs)
        scores, m = [], None
        for j in range(n):
            s = s_all[j * tq:(j + 1) * tq]
            if j == n - 1:
                key = lax.broadcasted_iota(jnp.int32, s.shape, 0)
                qry = lax.broadcasted_iota(jnp.int32, s.shape, 1)
                qry = jnp.where(qry >= tq, qry - tq, qry)
                s = jnp.where(key <= qry, s, -jnp.inf)
            mj = jnp.max(s, axis=0, keepdims=True)
            m = mj if m is None else jnp.maximum(m, mj)
            scores.append(s)
        return scores, m

    def finish(hh, scores, m):
        acc = jnp.zeros((dv, 2 * tq), F32)
        l = jnp.zeros((1, 2 * tq), F32)
        for j, s in enumerate(scores):
            e = jnp.exp2(s - m)
            l = l + jnp.sum(e, axis=0, keepdims=True)
            acc = acc + jnp.dot(vt_ref[hh, :, j * tq:(j + 1) * tq], e.astype(BF16),
                                preferred_element_type=F32)
        o = acc / l
        od = (o[:, :tq] - lam * o[:, tq:]).T
        o_ref[:, hh * dv:(hh + 1) * dv] = _sub_norm(od, subln_ref[...], lam_init).astype(o_ref.dtype)

    for n in range(1, n_tiles + 1):
        @pl.when(qi == n - 1)
        def _(n=n):
            pending = score(0, n)
            for hh in range(1, heads):
                upcoming = score(hh, n)
                finish(hh - 1, *pending)
                pending = upcoming
            finish(heads - 1, *pending)


def _prompt_attention(q, k, v, lam_params, subln, *, n_heads, head_dim, tq, heads, lam_init):
    b, s, _ = q.shape
    dv = v.shape[-1] // n_heads
    dqk = 2 * head_dim
    small = lambda n: pl.BlockSpec((1, n), lambda bi, h, i: (0, 0))
    est = (2 * heads * (s * (dqk + dv) * 4 + tq * (dqk + dv) * 2) + heads * dv * s * 2
           + heads * s * 2 * tq * 4 * 2)
    return pl.pallas_call(
        functools.partial(_prompt_attn_kernel, tq=tq, head_dim=head_dim, heads=heads, lam_init=lam_init),
        grid=(b, n_heads // heads, s // tq),
        in_specs=[small(head_dim)] * 4 + [small(dv),
                  pl.BlockSpec((None, tq, heads * dqk), lambda bi, h, i: (bi, i, h)),
                  pl.BlockSpec((None, s, heads * dqk), lambda bi, h, i: (bi, 0, h)),
                  pl.BlockSpec((None, s, heads * dv), lambda bi, h, i: (bi, 0, h))],
        out_specs=pl.BlockSpec((None, tq, heads * dv), lambda bi, h, i: (bi, i, h)),
        out_shape=jax.ShapeDtypeStruct((b, s, n_heads * dv), BF16),
        scratch_shapes=[pltpu.VMEM((heads, dv, s), BF16)],
        compiler_params=pltpu.CompilerParams(
            dimension_semantics=("parallel", "parallel", "arbitrary"),
            vmem_limit_bytes=_vmem_limit(est)),
        name="prompt_attn",
    )(*lam_params, subln, q, k, v)


def _sample_attn_kernel(pt_ref, lq1_ref, lk1_ref, lq2_ref, lk2_ref, subln_ref, q_ref, kn_ref, vn_ref,
                        *rest, pp, n_heads, head_dim, lam_init):
    del pt_ref
    k_refs, v_refs = rest[:pp], rest[pp:2 * pp]
    o_ref, qall_ref, vext_ref, m_ref, acc_ref = rest[2 * pp:]
    pg = pl.program_id(1)
    t = q_ref.shape[0]
    dqk = 2 * head_dim
    dv = vn_ref.shape[-1]
    n_rows = n_heads * 2 * t

    @pl.when(pg == 0)
    def _():
        qall_ref[...] = jnp.concatenate(
            [_split_halves(q_ref[:, h * dqk:(h + 1) * dqk], head_dim) for h in range(n_heads)],
            axis=0).astype(BF16)
        vext_ref[:, :, dv:] = jnp.ones((pp, vext_ref.shape[1], vext_ref.shape[2] - dv), BF16)
        m_ref[...] = jnp.full(m_ref.shape, NEG_BIG, F32)
        acc_ref[...] = jnp.zeros(acc_ref.shape, F32)

    qall = qall_ref[...]
    row = lax.broadcasted_iota(jnp.int32, (n_rows, LANES), 0)
    col = lax.broadcasted_iota(jnp.int32, (n_rows, LANES), 1)
    same_head = (col % n_heads) == (row // (2 * t))

    def fold_max(s):
        blocks = [s[:, c:c + LANES] for c in range(0, s.shape[1], LANES)]
        return functools.reduce(jnp.maximum, blocks)

    def update(scores, valid, values):
        m_old = m_ref[...]
        folded = functools.reduce(jnp.maximum, [fold_max(s) for s in scores])
        m_new = jnp.maximum(m_old, jnp.max(jnp.where(valid, folded, -jnp.inf), axis=-1, keepdims=True))
        shift = jnp.where(valid, m_new, jnp.inf)
        acc = acc_ref[...] * jnp.exp2(m_old - m_new)
        for s, v in zip(scores, values):
            e = jnp.exp2((s - jnp.tile(shift, (1, s.shape[1] // LANES))).astype(BF16))
            acc = acc + jnp.dot(e, v, preferred_element_type=F32)
        m_ref[...] = m_new
        acc_ref[...] = acc

    for i, r in enumerate(v_refs):
        vext_ref[i, :, 0:dv] = r[...].astype(BF16)
    scores = [_dot_nt(qall, r[...].astype(BF16)) for r in k_refs]
    for g in range(0, pp, PAGE_GROUP):
        update(scores[g:g + PAGE_GROUP], same_head, [vext_ref[i] for i in range(g, min(g + PAGE_GROUP, pp))])

    @pl.when(pg == pl.num_programs(1) - 1)
    def _():
        lam = _diff_lambda(lq1_ref, lk1_ref, lq2_ref, lk2_ref, lam_init)
        n_new = kn_ref.shape[0]
        pad = jnp.zeros((LANES - n_new, dqk), F32)
        kn = jnp.concatenate([kn_ref[...], pad], axis=0).astype(BF16)
        vn = jnp.concatenate([vn_ref[...], pad], axis=0).astype(BF16)
        vn = jnp.concatenate([vn, jnp.ones((LANES, acc_ref.shape[1] - dv), BF16)], axis=1)
        causal = (col // n_heads) <= (row % t)
        update([_dot_nt(qall, kn)], same_head & causal & (col < n_new), [vn])
        o = acc_ref[:, 0:dv] / acc_ref[:, dv:2 * dv]
        for h in range(n_heads):
            r0 = h * 2 * t
            od = o[r0:r0 + t] - lam * o[r0 + t:r0 + 2 * t]
            o_ref[:, h * dv:(h + 1) * dv] = _sub_norm(od, subln_ref[...], lam_init)


def _sample_attention(q, k_new, v_new, cache_k2, cache_v2, page_table, layer_base, lam_params, subln,
                      *, n_heads, head_dim, pp, lam_init):
    db, t, _ = q.shape
    n_pages = page_table.shape[1]
    rows = cache_k2.shape[1]
    dqk = 2 * head_dim
    dv = cache_v2.shape[-1]
    n_rows = n_heads * 2 * t
    small = lambda n: pl.BlockSpec((1, n), lambda b, p, pt: (0, 0))
    per_b = lambda r, n: pl.BlockSpec((None, r, n), lambda b, p, pt: (b, 0, 0))

    def page_spec(i, width):
        return pl.BlockSpec((None, rows, width), lambda b, p, pt: (layer_base + pt[b, p * pp + i], 0, 0))

    est = 2 * pp * rows * (dqk + dv) * 4 + pp * rows * 2 * dv * 2 + pp * n_rows * rows * 4 * 2
    return pl.pallas_call(
        functools.partial(_sample_attn_kernel, pp=pp, n_heads=n_heads, head_dim=head_dim,
                          lam_init=lam_init),
        grid_spec=pltpu.PrefetchScalarGridSpec(
            num_scalar_prefetch=1,
            grid=(db, n_pages // pp),
            in_specs=[small(head_dim)] * 4 + [small(dv), per_b(t, n_heads * dqk),
                                              per_b(t * n_heads, dqk), per_b(t * n_heads, dv)]
                     + [page_spec(i, dqk) for i in range(pp)] + [page_spec(i, dv) for i in range(pp)],
            out_specs=per_b(t, n_heads * dv),
            scratch_shapes=[pltpu.VMEM((n_rows, dqk), BF16), pltpu.VMEM((pp, rows, 2 * dv), BF16),
                            pltpu.VMEM((n_rows, 1), F32), pltpu.VMEM((n_rows, 2 * dv), F32)]),
        out_shape=jax.ShapeDtypeStruct((db, t, n_heads * dv), F32),
        compiler_params=pltpu.CompilerParams(
            dimension_semantics=("parallel", "arbitrary"), vmem_limit_bytes=_vmem_limit(est)),
        name="sample_attn",
    )(page_table, *lam_params, subln, q, k_new, v_new, *([cache_k2] * pp), *([cache_v2] * pp))


def _post_kernel(x_ref, o_ref, ca_ref, g_ref, hf_ref,
                 wa_ref, wb_ref, wout_ref, nmp_ref, nfp_ref, wup_ref, fcw_ref, fcb_ref, wdown_ref, nfq_ref,
                 y_ref, fs_ref, fhist_ref, *, nb, t, parts):
    ti = pl.program_id(1)
    rows = nb * t
    d = x_ref.shape[-1]
    ffn = fhist_ref.shape[-1]

    @pl.when(ti == 0)
    def _():
        fhist_ref[...] = hf_ref[...]

    def mix_stage(r):
        y_b = jnp.dot(ca_ref[r, :], wb_ref[...], preferred_element_type=F32)
        y_a = jnp.dot(o_ref[r, :].astype(BF16), wa_ref[...], preferred_element_type=F32)
        mix = g_ref[r, :d].astype(F32) * y_a + g_ref[r, d:].astype(F32) * y_b
        return x_ref[r, :] + _rms(jnp.dot(mix.astype(BF16), wout_ref[...], preferred_element_type=F32),
                                  nmp_ref[...])

    def ffn_stage(r, x1, hist):
        n = (r.stop - r.start) // nb
        h2 = _rms(x1, nfp_ref[...]).astype(BF16)
        fg = jnp.dot(h2, wup_ref[:, :ffn], preferred_element_type=F32)
        fu = jnp.dot(h2, wup_ref[:, ffn:], preferred_element_type=F32)
        fext = jnp.concatenate([hist, fg.reshape(nb, n, ffn)], axis=1)
        cv = _causal_dwconv(fext, fcw_ref, fcb_ref[...], t=n, pad=HIST_F_PAD)
        f = jax.nn.gelu(cv).reshape(nb * n, ffn) * fu
        down = jnp.dot(f.astype(BF16), wdown_ref[...], preferred_element_type=F32)
        y_ref[r, :] = x1 + _rms(down, nfq_ref[...])
        return fext[:, n:n + HIST_F_PAD, :]

    part_rows = [slice(p * (rows // parts), (p + 1) * (rows // parts)) for p in range(parts)]
    mixed = [mix_stage(r) for r in part_rows]
    tail = fhist_ref[...]
    for r, x1 in zip(part_rows, mixed):
        tail = ffn_stage(r, x1, tail)

    fs_ref[...] = tail
    fhist_ref[...] = tail


def _post(x2d, o2d, ca2d, g2d, hist_f, w, *, n_seq, nb, t):
    m, d = x2d.shape
    dv = o2d.shape[1]
    d_c = ca2d.shape[1]
    ffn = w["w_down"].shape[0]
    tiles = (m // n_seq) // t
    rows = nb * t
    row2 = lambda n: pl.BlockSpec((rows, n), lambda b, i: (b * tiles + i, 0))
    hist = pl.BlockSpec((nb, HIST_F_PAD, ffn), lambda b, i: (b, 0, 0))
    weights = [w["w_a"], w["w_b"], w["w_out"], w["norm_mix_post"], w["norm_ffn_pre"], w["w_up"],
               w["ffn_conv_w"], w["ffn_conv_b"], w["w_down"], w["norm_ffn_post"]]
    w_bytes = sum(int(a.size) * a.dtype.itemsize for a in weights)
    est = (w_bytes + 2 * rows * (d * 4 * 2 + dv * o2d.dtype.itemsize + d_c * 2 + 2 * d * 2)
           + nb * HIST_F_PAD * ffn * 4 + 10 * rows * d * 4 + 6 * rows * ffn * 4)
    return pl.pallas_call(
        functools.partial(_post_kernel, nb=nb, t=t, parts=POST_ROW_PARTS if nb == 1 else 1),
        grid=(n_seq // nb, tiles),
        in_specs=[row2(d), row2(dv), row2(d_c), row2(2 * d), hist] + [_resident(a.shape) for a in weights],
        out_specs=[row2(d), hist],
        out_shape=[jax.ShapeDtypeStruct(x2d.shape, F32), jax.ShapeDtypeStruct((n_seq, HIST_F_PAD, ffn), F32)],
        scratch_shapes=[pltpu.VMEM((nb, HIST_F_PAD, ffn), F32)],
        compiler_params=pltpu.CompilerParams(
            dimension_semantics=("parallel", "arbitrary"), vmem_limit_bytes=_vmem_limit(est)),
        name="post",
    )(x2d, o2d, ca2d, g2d, hist_f, *weights)


def _pad_history(hist, rows):
    return jnp.pad(hist, ((0, 0), (rows - hist.shape[1], 0), (0, 0)))


def kernel(x_prompt, x_sample, cache_k, cache_v, state_conv, state_ffn, page_table, norm_mix_pre, w_in, b_glu, b_gate, lambda_q1, lambda_k1, lambda_q2, lambda_k2, subln, w_a, conv_w, conv_b, ln_g, ln_b, w_b, w_out, norm_mix_post, norm_ffn_pre, w_up, ffn_conv_w, ffn_conv_b, w_down, norm_ffn_post):
    depth = w_in.shape[0]
    bp, sp, d = x_prompt.shape
    bs, ts, _ = x_sample.shape
    _, n_pool, page, n_heads, dqk = cache_k.shape
    head_dim = dqk // 2
    dv = cache_v.shape[-1]
    d_qk, d_v = n_heads * dqk, n_heads * dv
    cch = conv_w.shape[-1]
    conv_k = conv_w.shape[1]
    ffn = w_down.shape[1]
    ffn_k = ffn_conv_w.shape[1]
    scale = head_dim ** -0.5 * LOG2_E

    cache_k2 = cache_k.reshape(depth * n_pool, page * n_heads, dqk)
    cache_v2 = cache_v.reshape(depth * n_pool, page * n_heads, dv)

    yp = x_prompt.reshape(bp * sp, d)
    ys = x_sample.reshape(bs * ts, d)
    zeros_c = jnp.zeros((bp, HIST_C_PAD, cch), F32)
    zeros_f = jnp.zeros((bp, HIST_F_PAD, ffn), F32)
    outs = [[] for _ in range(8)]
    for l in range(depth):
        lam_init = 0.8 - 0.6 * math.exp(-0.3 * l)
        row = lambda a: a[l].reshape(1, -1)
        lam_params = [row(lambda_q1), row(lambda_k1), row(lambda_q2), row(lambda_k2)]
        w_in_bf = w_in[l].astype(BF16)
        w = {
            "norm_mix_pre": row(norm_mix_pre), "w_in": w_in_bf, "b_glu": row(b_glu), "b_gate": row(b_gate),
            "conv_w": conv_w[l], "conv_b": row(conv_b), "ln_g": row(ln_g), "ln_b": row(ln_b),
            "w_a": w_a[l].astype(BF16), "w_b": w_b[l].astype(BF16), "w_out": w_out[l].astype(BF16),
            "norm_mix_post": row(norm_mix_post), "norm_ffn_pre": row(norm_ffn_pre),
            "w_up": w_up[l].astype(BF16), "ffn_conv_w": ffn_conv_w[l], "ffn_conv_b": row(ffn_conv_b),
            "w_down": w_down[l].astype(BF16), "norm_ffn_post": row(norm_ffn_post),
        }
        proj_kw = dict(d_qk=d_qk, d_v=d_v, scale=scale)

        q, k, v, g, ca, cs = _proj_in(yp, zeros_c, w, n_seq=bp, nb=1, t=512, q_dtype=BF16, **proj_kw)
        o = _prompt_attention(q.reshape(bp, sp, d_qk), k.reshape(bp, sp, d_qk), v.reshape(bp, sp, d_v),
                              lam_params, row(subln), n_heads=n_heads, head_dim=head_dim, tq=256, heads=8,
                              lam_init=lam_init)
        yp, fs = _post(yp, o.reshape(bp * sp, d_v), ca, g, zeros_f, w, n_seq=bp, nb=1, t=512)
        outs[0].append(k.reshape(bp, sp, n_heads, dqk))
        outs[1].append(v.reshape(bp, sp, n_heads, dv))
        outs[2].append(cs[:, HIST_C_PAD - (conv_k - 1):])
        outs[3].append(fs[:, HIST_F_PAD - (ffn_k - 1):])

        q, k, v, g, ca, cs = _proj_in(ys, _pad_history(state_conv[l], HIST_C_PAD), w, n_seq=bs, nb=bs, t=ts,
                                      q_dtype=F32, **proj_kw)
        k_rows = k.reshape(bs, ts, n_heads, dqk)
        v_rows = v.reshape(bs, ts, n_heads, dv)
        o = _sample_attention(q.reshape(bs, ts, d_qk), k_rows.reshape(bs, ts * n_heads, dqk),
                              v_rows.reshape(bs, ts * n_heads, dv), cache_k2, cache_v2, page_table,
                              l * n_pool, lam_params, row(subln), n_heads=n_heads, head_dim=head_dim,
                              pp=16, lam_init=lam_init)
        ys, fs = _post(ys, o.reshape(bs * ts, d_v), ca, g, _pad_history(state_ffn[l], HIST_F_PAD), w,
                       n_seq=bs, nb=bs, t=ts)
        outs[4].append(k_rows)
        outs[5].append(v_rows)
        outs[6].append(cs[:, HIST_C_PAD - (conv_k - 1):])
        outs[7].append(fs[:, HIST_F_PAD - (ffn_k - 1):])

    return (yp.reshape(bp, sp, d), ys.reshape(bs, ts, d)) + tuple(jnp.stack(o) for o in outs)
```

```python
import functools
import math

import jax
import jax.numpy as jnp
from jax import lax
from jax.experimental import pallas as pl
from jax.experimental.pallas import tpu as pltpu

F32 = jnp.float32
BF16 = jnp.bfloat16

V7X_VMEM_BYTES = 64 * 1024 * 1024
LANES = 128
SUBLANES = 8

EPS = 1e-6
LN_EPS = 1e-5
NEG_BIG = -1e30

HIST_C_PAD = 32
HIST_F_PAD = 8
POST_ROW_PARTS = 2
LOG2_E = math.log2(math.e)
PAGE_GROUP = 4

def _vmem_limit(estimate_bytes):
    return int(min(V7X_VMEM_BYTES - 4 * 1024 * 1024, max(estimate_bytes, 16 * 1024 * 1024)))


def _resident(shape):
    nd = len(shape)
    return pl.BlockSpec(shape, lambda *_: (0,) * nd, pipeline_mode=pl.Buffered(1))


def _rms(x, gain):
    return x * lax.rsqrt(jnp.mean(x * x, axis=-1, keepdims=True) + EPS) * gain


def _diff_lambda(lq1_ref, lk1_ref, lq2_ref, lk2_ref, lam_init):
    a = jnp.sum(lq1_ref[...] * lk1_ref[...], axis=-1, keepdims=True)
    b = jnp.sum(lq2_ref[...] * lk2_ref[...], axis=-1, keepdims=True)
    return jnp.exp(a) - jnp.exp(b) + lam_init


def _dot_nt(a, b):
    return lax.dot_general(a, b, (((1,), (1,)), ((), ())), preferred_element_type=F32)


def _split_halves(qh, head_dim):
    lane = lax.broadcasted_iota(jnp.int32, qh.shape, 1)
    zero = jnp.zeros_like(qh)
    return jnp.concatenate([jnp.where(lane < head_dim, qh, zero),
                            jnp.where(lane >= head_dim, qh, zero)], axis=0)


def _sub_norm(o, subln, lam_init):
    return _rms(o, subln) * (1.0 - lam_init)


def _causal_dwconv(ext, w_ref, bias, *, t, pad):
    k = w_ref.shape[0]
    base = pad - (k - 1)
    y = None
    for r in range(SUBLANES):
        taps = [j for j in range(k) if (base + j) % SUBLANES == r]
        if not taps:
            continue
        span = t if r == 0 else t + SUBLANES
        u = None
        for j in taps:
            a0 = base + j - r
            term = ext[:, a0:a0 + span, :] * w_ref[pl.ds(j, 1), :]
            u = term if u is None else u + term
        u = u if r == 0 else u[:, r:r + t, :]
        y = u if y is None else y + u
    return y + bias


def _proj_in_kernel(x_ref, gain_ref, w_ref, bglu_ref, bgate_ref, hc_ref, cw_ref, cb_ref, lng_ref, lnb_ref,
                    q_ref, k_ref, v_ref, g_ref, ca_ref, cs_ref, cext_ref, *, nb, t, d_qk, d_v, scale):
    ti = pl.program_id(1)
    d_c = ca_ref.shape[-1]
    d_g = g_ref.shape[-1]
    h = _rms(x_ref[...], gain_ref[...]).astype(BF16)

    def proj(c0, n):
        return jnp.dot(h, w_ref[:, c0:c0 + n], preferred_element_type=F32)

    @pl.when(ti == 0)
    def _():
        cext_ref[:, 0:HIST_C_PAD, :] = hc_ref[...]

    c0 = 2 * d_qk + d_v
    glu_a = proj(c0, d_c) + bglu_ref[:, :d_c]
    glu_b = proj(c0 + d_c, d_c) + bglu_ref[:, d_c:]
    cext_ref[:, HIST_C_PAD:HIST_C_PAD + t, :] = (glu_a * jax.nn.sigmoid(glu_b)).reshape(nb, t, d_c)

    q_ref[...] = (proj(0, d_qk) * scale).astype(q_ref.dtype)
    k_ref[...] = proj(d_qk, d_qk)
    v_ref[...] = proj(2 * d_qk, d_v)
    half = d_g // 2
    for i in range(2):
        gates = proj(c0 + 2 * d_c + i * half, half) + bgate_ref[:, i * half:(i + 1) * half]
        g_ref[:, i * half:(i + 1) * half] = jax.nn.sigmoid(gates).astype(g_ref.dtype)

    conv = _causal_dwconv(cext_ref, cw_ref, cb_ref[...], t=t, pad=HIST_C_PAD)
    mu = jnp.mean(conv, axis=-1, keepdims=True)
    xc = conv - mu
    var = jnp.mean(xc * xc, axis=-1, keepdims=True)
    ln = xc * lax.rsqrt(var + LN_EPS) * lng_ref[...] + lnb_ref[...]
    ca_ref[...] = jax.nn.silu(ln).reshape(nb * t, d_c).astype(ca_ref.dtype)

    tail = cext_ref[:, t:t + HIST_C_PAD, :]
    cs_ref[...] = tail
    cext_ref[:, 0:HIST_C_PAD, :] = tail


def _proj_in(x2d, hist_c, w, *, n_seq, nb, t, q_dtype, d_qk, d_v, scale):
    m, d = x2d.shape
    tiles = (m // n_seq) // t
    rows = nb * t
    n_in = w["w_in"].shape[1]
    d_c = w["conv_w"].shape[1]
    d_g = n_in - 2 * d_qk - d_v - 2 * d_c
    row = lambda n: pl.BlockSpec((rows, n), lambda b, i: (b * tiles + i, 0))
    hist = pl.BlockSpec((nb, HIST_C_PAD, d_c), lambda b, i: (b, 0, 0))
    weights = [w["norm_mix_pre"], w["w_in"], w["b_glu"], w["b_gate"]]
    conv_weights = [w["conv_w"], w["conv_b"], w["ln_g"], w["ln_b"]]
    est = (d * n_in * 2 + 2 * rows * (d * 4 + d_qk * 8 + d_v * 4 + d_c * 2 + d_g * 2)
           + nb * (HIST_C_PAD + t) * d_c * 4 + 8 * rows * max(d_qk, d_g // 2) * 4)
    return pl.pallas_call(
        functools.partial(_proj_in_kernel, nb=nb, t=t, d_qk=d_qk, d_v=d_v, scale=scale),
        grid=(n_seq // nb, tiles),
        in_specs=([row(d)] + [_resident(a.shape) for a in weights] + [hist]
                  + [_resident(a.shape) for a in conv_weights]),
        out_specs=[row(d_qk), row(d_qk), row(d_v), row(d_g), row(d_c), hist],
        out_shape=[jax.ShapeDtypeStruct((m, d_qk), q_dtype), jax.ShapeDtypeStruct((m, d_qk), F32),
                   jax.ShapeDtypeStruct((m, d_v), F32), jax.ShapeDtypeStruct((m, d_g), BF16),
                   jax.ShapeDtypeStruct((m, d_c), BF16),
                   jax.ShapeDtypeStruct((n_seq, HIST_C_PAD, d_c), F32)],
        scratch_shapes=[pltpu.VMEM((nb, HIST_C_PAD + t, d_c), F32)],
        compiler_params=pltpu.CompilerParams(
            dimension_semantics=("parallel", "arbitrary"), vmem_limit_bytes=_vmem_limit(est)),
        name="proj_in",
    )(x2d, *weights, hist_c, *conv_weights)


def _prompt_attn_kernel(lq1_ref, lk1_ref, lq2_ref, lk2_ref, subln_ref, q_ref, k_ref, v_ref, o_ref, vt_ref,
                        *, tq, head_dim, heads, lam_init):
    qi = pl.program_id(2)
    dqk = 2 * head_dim
    dv = vt_ref.shape[1]
    n_tiles = k_ref.shape[0] // tq

    @pl.when(qi == 0)
    def _():
        eye = (lax.broadcasted_iota(jnp.int32, (dv, dv), 0)
               == lax.broadcasted_iota(jnp.int32, (dv, dv), 1)).astype(BF16)
        for hh in range(heads):
            vt_ref[hh] = _dot_nt(eye, v_ref[:, hh * dv:(hh + 1) * dv].astype(BF16)).astype(BF16)

    lam = _diff_lambda(lq1_ref, lk1_ref, lq2_ref, lk2_ref, lam_init)

    def score(hh, n):
        qs = _split_halves(q_ref[:, hh * dqk:(hh + 1) * dqk], head_dim)
        kb = k_ref[0:n * tq, hh * dqk:(hh + 1) * dqk].astype(BF16)
        s_all = _dot_nt(kb, qs)
        scores, m = [], None
        for j in range(n):
            s = s_all[j * tq:(j + 1) * tq]
            if j == n - 1:
                key = lax.broadcasted_iota(jnp.int32, s.shape, 0)
                qry = lax.broadcasted_iota(jnp.int32, s.shape, 1)
                qry = jnp.where(qry >= tq, qry - tq, qry)
                s = jnp.where(key <= qry, s, -jnp.inf)
            mj = jnp.max(s, axis=0, keepdims=True)
            m = mj if m is None else jnp.maximum(m, mj)
            scores.append(s)
        return scores, m

    def finish(hh, scores, m):
        acc = jnp.zeros((dv, 2 * tq), F32)
        l = jnp.zeros((1, 2 * tq), F32)
        for j, s in enumerate(scores):
            e = jnp.exp2(s - m)
            l = l + jnp.sum(e, axis=0, keepdims=True)
            acc = acc + jnp.dot(vt_ref[hh, :, j * tq:(j + 1) * tq], e.astype(BF16),
                                preferred_element_type=F32)
        o = acc / l
        od = (o[:, :tq] - lam * o[:, tq:]).T
        o_ref[:, hh * dv:(hh + 1) * dv] = _sub_norm(od, subln_ref[...], lam_init).astype(o_ref.dtype)

    for n in range(1, n_tiles + 1):
        @pl.when(qi == n - 1)
        def _(n=n):
            pending = score(0, n)
            for hh in range(1, heads):
                upcoming = score(hh, n)
                finish(hh - 1, *pending)
                pending = upcoming
            finish(heads - 1, *pending)


def _prompt_attention(q, k, v, lam_params, subln, *, n_heads, head_dim, tq, heads, lam_init):
    b, s, _ = q.shape
    dv = v.shape[-1] // n_heads
    dqk = 2 * head_dim
    small = lambda n: pl.BlockSpec((1, n), lambda bi, h, i: (0, 0))
    est = (2 * heads * (s * (dqk + dv) * 4 + tq * (dqk + dv) * 2) + heads * dv * s * 2
           + heads * s * 2 * tq * 4 * 2)
    return pl.pallas_call(
        functools.partial(_prompt_attn_kernel, tq=tq, head_dim=head_dim, heads=heads, lam_init=lam_init),
        grid=(b, n_heads // heads, s // tq),
        in_specs=[small(head_dim)] * 4 + [small(dv),
                  pl.BlockSpec((None, tq, heads * dqk), lambda bi, h, i: (bi, i, h)),
                  pl.BlockSpec((None, s, heads * dqk), lambda bi, h, i: (bi, 0, h)),
                  pl.BlockSpec((None, s, heads * dv), lambda bi, h, i: (bi, 0, h))],
        out_specs=pl.BlockSpec((None, tq, heads * dv), lambda bi, h, i: (bi, i, h)),
        out_shape=jax.ShapeDtypeStruct((b, s, n_heads * dv), BF16),
        scratch_shapes=[pltpu.VMEM((heads, dv, s), BF16)],
        compiler_params=pltpu.CompilerParams(
            dimension_semantics=("parallel", "parallel", "arbitrary"),
            vmem_limit_bytes=_vmem_limit(est)),
        name="prompt_attn",
    )(*lam_params, subln, q, k, v)


def _sample_attn_kernel(pt_ref, lq1_ref, lk1_ref, lq2_ref, lk2_ref, subln_ref, q_ref, kn_ref, vn_ref,
                        *rest, pp, n_heads, head_dim, lam_init):
    del pt_ref
    k_refs, v_refs = rest[:pp], rest[pp:2 * pp]
    o_ref, qall_ref, vext_ref, m_ref, acc_ref = rest[2 * pp:]
    pg = pl.program_id(1)
    t = q_ref.shape[0]
    dqk = 2 * head_dim
    dv = vn_ref.shape[-1]
    n_rows = n_heads * 2 * t

    @pl.when(pg == 0)
    def _():
        qall_ref[...] = jnp.concatenate(
            [_split_halves(q_ref[:, h * dqk:(h + 1) * dqk], head_dim) for h in range(n_heads)],
            axis=0).astype(BF16)
        vext_ref[:, :, dv:] = jnp.ones((pp, vext_ref.shape[1], vext_ref.shape[2] - dv), BF16)
        m_ref[...] = jnp.full(m_ref.shape, NEG_BIG, F32)
        acc_ref[...] = jnp.zeros(acc_ref.shape, F32)

    qall = qall_ref[...]
    row = lax.broadcasted_iota(jnp.int32, (n_rows, LANES), 0)
    col = lax.broadcasted_iota(jnp.int32, (n_rows, LANES), 1)
    same_head = (col % n_heads) == (row // (2 * t))

    def fold_max(s):
        blocks = [s[:, c:c + LANES] for c in range(0, s.shape[1], LANES)]
        return functools.reduce(jnp.maximum, blocks)

    def update(scores, valid, values):
        m_old = m_ref[...]
        folded = functools.reduce(jnp.maximum, [fold_max(s) for s in scores])
        m_new = jnp.maximum(m_old, jnp.max(jnp.where(valid, folded, -jnp.inf), axis=-1, keepdims=True))
        shift = jnp.where(valid, m_new, jnp.inf)
        acc = acc_ref[...] * jnp.exp2(m_old - m_new)
        for s, v in zip(scores, values):
            e = jnp.exp2((s - jnp.tile(shift, (1, s.shape[1] // LANES))).astype(BF16))
            acc = acc + jnp.dot(e, v, preferred_element_type=F32)
        m_ref[...] = m_new
        acc_ref[...] = acc

    for i, r in enumerate(v_refs):
        vext_ref[i, :, 0:dv] = r[...].astype(BF16)
    scores = [_dot_nt(qall, r[...].astype(BF16)) for r in k_refs]
    for g in range(0, pp, PAGE_GROUP):
        update(scores[g:g + PAGE_GROUP], same_head, [vext_ref[i] for i in range(g, min(g + PAGE_GROUP, pp))])

    @pl.when(pg == pl.num_programs(1) - 1)
    def _():
        lam = _diff_lambda(lq1_ref, lk1_ref, lq2_ref, lk2_ref, lam_init)
        n_new = kn_ref.shape[0]
        pad = jnp.zeros((LANES - n_new, dqk), F32)
        kn = jnp.concatenate([kn_ref[...], pad], axis=0).astype(BF16)
        vn = jnp.concatenate([vn_ref[...], pad], axis=0).astype(BF16)
        vn = jnp.concatenate([vn, jnp.ones((LANES, acc_ref.shape[1] - dv), BF16)], axis=1)
        causal = (col // n_heads) <= (row % t)
        update([_dot_nt(qall, kn)], same_head & causal & (col < n_new), [vn])
        o = acc_ref[:, 0:dv] / acc_ref[:, dv:2 * dv]
        for h in range(n_heads):
            r0 = h * 2 * t
            od = o[r0:r0 + t] - lam * o[r0 + t:r0 + 2 * t]
            o_ref[:, h * dv:(h + 1) * dv] = _sub_norm(od, subln_ref[...], lam_init)


def _sample_attention(q, k_new, v_new, cache_k2, cache_v2, page_table, layer_base, lam_params, subln,
                      *, n_heads, head_dim, pp, lam_init):
    db, t, _ = q.shape
    n_pages = page_table.shape[1]
    rows = cache_k2.shape[1]
    dqk = 2 * head_dim
    dv = cache_v2.shape[-1]
    n_rows = n_heads * 2 * t
    small = lambda n: pl.BlockSpec((1, n), lambda b, p, pt: (0, 0))
    per_b = lambda r, n: pl.BlockSpec((None, r, n), lambda b, p, pt: (b, 0, 0))

    def page_spec(i, width):
        return pl.BlockSpec((None, rows, width), lambda b, p, pt: (layer_base + pt[b, p * pp + i], 0, 0))

    est = 2 * pp * rows * (dqk + dv) * 4 + pp * rows * 2 * dv * 2 + pp * n_rows * rows * 4 * 2
    return pl.pallas_call(
        functools.partial(_sample_attn_kernel, pp=pp, n_heads=n_heads, head_dim=head_dim,
                          lam_init=lam_init),
        grid_spec=pltpu.PrefetchScalarGridSpec(
            num_scalar_prefetch=1,
            grid=(db, n_pages // pp),
            in_specs=[small(head_dim)] * 4 + [small(dv), per_b(t, n_heads * dqk),
                                              per_b(t * n_heads, dqk), per_b(t * n_heads, dv)]
                     + [page_spec(i, dqk) for i in range(pp)] + [page_spec(i, dv) for i in range(pp)],
            out_specs=per_b(t, n_heads * dv),
            scratch_shapes=[pltpu.VMEM((n_rows, dqk), BF16), pltpu.VMEM((pp, rows, 2 * dv), BF16),
                            pltpu.VMEM((n_rows, 1), F32), pltpu.VMEM((n_rows, 2 * dv), F32)]),
        out_shape=jax.ShapeDtypeStruct((db, t, n_heads * dv), F32),
        compiler_params=pltpu.CompilerParams(
            dimension_semantics=("parallel", "arbitrary"), vmem_limit_bytes=_vmem_limit(est)),
        name="sample_attn",
    )(page_table, *lam_params, subln, q, k_new, v_new, *([cache_k2] * pp), *([cache_v2] * pp))


def _post_kernel(x_ref, o_ref, ca_ref, g_ref, hf_ref,
                 wa_ref, wb_ref, wout_ref, nmp_ref, nfp_ref, wup_ref, fcw_ref, fcb_ref, wdown_ref, nfq_ref,
                 y_ref, fs_ref, fhist_ref, *, nb, t, parts):
    ti = pl.program_id(1)
    rows = nb * t
    d = x_ref.shape[-1]
    ffn = fhist_ref.shape[-1]

    @pl.when(ti == 0)
    def _():
        fhist_ref[...] = hf_ref[...]

    def mix_stage(r):
        y_b = jnp.dot(ca_ref[r, :], wb_ref[...], preferred_element_type=F32)
        y_a = jnp.dot(o_ref[r, :].astype(BF16), wa_ref[...], preferred_element_type=F32)
        mix = g_ref[r, :d].astype(F32) * y_a + g_ref[r, d:].astype(F32) * y_b
        return x_ref[r, :] + _rms(jnp.dot(mix.astype(BF16), wout_ref[...], preferred_element_type=F32),
                                  nmp_ref[...])

    def ffn_stage(r, x1, hist):
        n = (r.stop - r.start) // nb
        h2 = _rms(x1, nfp_ref[...]).astype(BF16)
        fg = jnp.dot(h2, wup_ref[:, :ffn], preferred_element_type=F32)
        fu = jnp.dot(h2, wup_ref[:, ffn:], preferred_element_type=F32)
        fext = jnp.concatenate([hist, fg.reshape(nb, n, ffn)], axis=1)
        cv = _causal_dwconv(fext, fcw_ref, fcb_ref[...], t=n, pad=HIST_F_PAD)
        f = jax.nn.gelu(cv).reshape(nb * n, ffn) * fu
        down = jnp.dot(f.astype(BF16), wdown_ref[...], preferred_element_type=F32)
        y_ref[r, :] = x1 + _rms(down, nfq_ref[...])
        return fext[:, n:n + HIST_F_PAD, :]

    part_rows = [slice(p * (rows // parts), (p + 1) * (rows // parts)) for p in range(parts)]
    mixed = [mix_stage(r) for r in part_rows]
    tail = fhist_ref[...]
    for r, x1 in zip(part_rows, mixed):
        tail = ffn_stage(r, x1, tail)

    fs_ref[...] = tail
    fhist_ref[...] = tail


def _post(x2d, o2d, ca2d, g2d, hist_f, w, *, n_seq, nb, t):
    m, d = x2d.shape
    dv = o2d.shape[1]
    d_c = ca2d.shape[1]
    ffn = w["w_down"].shape[0]
    tiles = (m // n_seq) // t
    rows = nb * t
    row2 = lambda n: pl.BlockSpec((rows, n), lambda b, i: (b * tiles + i, 0))
    hist = pl.BlockSpec((nb, HIST_F_PAD, ffn), lambda b, i: (b, 0, 0))
    weights = [w["w_a"], w["w_b"], w["w_out"], w["norm_mix_post"], w["norm_ffn_pre"], w["w_up"],
               w["ffn_conv_w"], w["ffn_conv_b"], w["w_down"], w["norm_ffn_post"]]
    w_bytes = sum(int(a.size) * a.dtype.itemsize for a in weights)
    est = (w_bytes + 2 * rows * (d * 4 * 2 + dv * o2d.dtype.itemsize + d_c * 2 + 2 * d * 2)
           + nb * HIST_F_PAD * ffn * 4 + 10 * rows * d * 4 + 6 * rows * ffn * 4)
    return pl.pallas_call(
        functools.partial(_post_kernel, nb=nb, t=t, parts=POST_ROW_PARTS if nb == 1 else 1),
        grid=(n_seq // nb, tiles),
        in_specs=[row2(d), row2(dv), row2(d_c), row2(2 * d), hist] + [_resident(a.shape) for a in weights],
        out_specs=[row2(d), hist],
        out_shape=[jax.ShapeDtypeStruct(x2d.shape, F32), jax.ShapeDtypeStruct((n_seq, HIST_F_PAD, ffn), F32)],
        scratch_shapes=[pltpu.VMEM((nb, HIST_F_PAD, ffn), F32)],
        compiler_params=pltpu.CompilerParams(
            dimension_semantics=("parallel", "arbitrary"), vmem_limit_bytes=_vmem_limit(est)),
        name="post",
    )(x2d, o2d, ca2d, g2d, hist_f, *weights)


def _pad_history(hist, rows):
    return jnp.pad(hist, ((0, 0), (rows - hist.shape[1], 0), (0, 0)))


def kernel(x_prompt, x_sample, cache_k, cache_v, state_conv, state_ffn, page_table, norm_mix_pre, w_in, b_glu, b_gate, lambda_q1, lambda_k1, lambda_q2, lambda_k2, subln, w_a, conv_w, conv_b, ln_g, ln_b, w_b, w_out, norm_mix_post, norm_ffn_pre, w_up, ffn_conv_w, ffn_conv_b, w_down, norm_ffn_post):
    depth = w_in.shape[0]
    bp, sp, d = x_prompt.shape
    bs, ts, _ = x_sample.shape
    _, n_pool, page, n_heads, dqk = cache_k.shape
    head_dim = dqk // 2
    dv = cache_v.shape[-1]
    d_qk, d_v = n_heads * dqk, n_heads * dv
    cch = conv_w.shape[-1]
    conv_k = conv_w.shape[1]
    ffn = w_down.shape[1]
    ffn_k = ffn_conv_w.shape[1]
    scale = head_dim ** -0.5 * LOG2_E

    cache_k2 = cache_k.reshape(depth * n_pool, page * n_heads, dqk)
    cache_v2 = cache_v.reshape(depth * n_pool, page * n_heads, dv)

    yp = x_prompt.reshape(bp * sp, d)
    ys = x_sample.reshape(bs * ts, d)
    zeros_c = jnp.zeros((bp, HIST_C_PAD, cch), F32)
    zeros_f = jnp.zeros((bp, HIST_F_PAD, ffn), F32)
    outs = [[] for _ in range(8)]
    for l in range(depth):
        lam_init = 0.8 - 0.6 * math.exp(-0.3 * l)
        row = lambda a: a[l].reshape(1, -1)
        lam_params = [row(lambda_q1), row(lambda_k1), row(lambda_q2), row(lambda_k2)]
        w_in_bf = w_in[l].astype(BF16)
        w = {
            "norm_mix_pre": row(norm_mix_pre), "w_in": w_in_bf, "b_glu": row(b_glu), "b_gate": row(b_gate),
            "conv_w": conv_w[l], "conv_b": row(conv_b), "ln_g": row(ln_g), "ln_b": row(ln_b),
            "w_a": w_a[l].astype(BF16), "w_b": w_b[l].astype(BF16), "w_out": w_out[l].astype(BF16),
            "norm_mix_post": row(norm_mix_post), "norm_ffn_pre": row(norm_ffn_pre),
            "w_up": w_up[l].astype(BF16), "ffn_conv_w": ffn_conv_w[l], "ffn_conv_b": row(ffn_conv_b),
            "w_down": w_down[l].astype(BF16), "norm_ffn_post": row(norm_ffn_post),
        }
        proj_kw = dict(d_qk=d_qk, d_v=d_v, scale=scale)

        q, k, v, g, ca, cs = _proj_in(yp, zeros_c, w, n_seq=bp, nb=1, t=512, q_dtype=BF16, **proj_kw)
        o = _prompt_attention(q.reshape(bp, sp, d_qk), k.reshape(bp, sp, d_qk), v.reshape(bp, sp, d_v),
                              lam_params, row(subln), n_heads=n_heads, head_dim=head_dim, tq=256, heads=4,
                              lam_init=lam_init)
        yp, fs = _post(yp, o.reshape(bp * sp, d_v), ca, g, zeros_f, w, n_seq=bp, nb=1, t=512)
        outs[0].append(k.reshape(bp, sp, n_heads, dqk))
        outs[1].append(v.reshape(bp, sp, n_heads, dv))
        outs[2].append(cs[:, HIST_C_PAD - (conv_k - 1):])
        outs[3].append(fs[:, HIST_F_PAD - (ffn_k - 1):])

        q, k, v, g, ca, cs = _proj_in(ys, _pad_history(state_conv[l], HIST_C_PAD), w, n_seq=bs, nb=bs, t=ts,
                                      q_dtype=F32, **proj_kw)
        k_rows = k.reshape(bs, ts, n_heads, dqk)
        v_rows = v.reshape(bs, ts, n_heads, dv)
        o = _sample_attention(q.reshape(bs, ts, d_qk), k_rows.reshape(bs, ts * n_heads, dqk),
                              v_rows.reshape(bs, ts * n_heads, dv), cache_k2, cache_v2, page_table,
                              l * n_pool, lam_params, row(subln), n_heads=n_heads, head_dim=head_dim,
                              pp=16, lam_init=lam_init)
        ys, fs = _post(ys, o.reshape(bs * ts, d_v), ca, g, _pad_history(state_ffn[l], HIST_F_PAD), w,
                       n_seq=bs, nb=bs, t=ts)
        outs[4].append(k_rows)
        outs[5].append(v_rows)
        outs[6].append(cs[:, HIST_C_PAD - (conv_k - 1):])
        outs[7].append(fs[:, HIST_F_PAD - (ffn_k - 1):])

    return (yp.reshape(bp, sp, d), ys.reshape(bs, ts, d)) + tuple(jnp.stack(o) for o in outs)
```

```python
import functools
import math

import jax
import jax.numpy as jnp
from jax import lax
from jax.experimental import pallas as pl
from jax.experimental.pallas import tpu as pltpu

F32 = jnp.float32
BF16 = jnp.bfloat16

V7X_VMEM_BYTES = 64 * 1024 * 1024
LANES = 128
SUBLANES = 8

EPS = 1e-6
LN_EPS = 1e-5
NEG_BIG = -1e30

HIST_C_PAD = 32
HIST_F_PAD = 8
POST_ROW_PARTS = 2
LOG2_E = math.log2(math.e)
PAGE_GROUP = 4

def _vmem_limit(estimate_bytes):
    return int(min(V7X_VMEM_BYTES - 4 * 1024 * 1024, max(estimate_bytes, 16 * 1024 * 1024)))


def _resident(shape):
    nd = len(shape)
    return pl.BlockSpec(shape, lambda *_: (0,) * nd, pipeline_mode=pl.Buffered(1))


def _rms(x, gain):
    return x * lax.rsqrt(jnp.mean(x * x, axis=-1, keepdims=True) + EPS) * gain


def _diff_lambda(lq1_ref, lk1_ref, lq2_ref, lk2_ref, lam_init):
    a = jnp.sum(lq1_ref[...] * lk1_ref[...], axis=-1, keepdims=True)
    b = jnp.sum(lq2_ref[...] * lk2_ref[...], axis=-1, keepdims=True)
    return jnp.exp(a) - jnp.exp(b) + lam_init


def _dot_nt(a, b):
    return lax.dot_general(a, b, (((1,), (1,)), ((), ())), preferred_element_type=F32)


def _split_halves(qh, head_dim):
    lane = lax.broadcasted_iota(jnp.int32, qh.shape, 1)
    zero = jnp.zeros_like(qh)
    return jnp.concatenate([jnp.where(lane < head_dim, qh, zero),
                            jnp.where(lane >= head_dim, qh, zero)], axis=0)


def _sub_norm(o, subln, lam_init):
    return _rms(o, subln) * (1.0 - lam_init)


def _causal_dwconv(ext, w_ref, bias, *, t, pad):
    k = w_ref.shape[0]
    base = pad - (k - 1)
    y = None
    for r in range(SUBLANES):
        taps = [j for j in range(k) if (base + j) % SUBLANES == r]
        if not taps:
            continue
        span = t if r == 0 else t + SUBLANES
        u = None
        for j in taps:
            a0 = base + j - r
            term = ext[:, a0:a0 + span, :] * w_ref[pl.ds(j, 1), :]
            u = term if u is None else u + term
        u = u if r == 0 else u[:, r:r + t, :]
        y = u if y is None else y + u
    return y + bias


def _proj_in_kernel(x_ref, gain_ref, w_ref, bglu_ref, bgate_ref, hc_ref, cw_ref, cb_ref, lng_ref, lnb_ref,
                    q_ref, k_ref, v_ref, g_ref, ca_ref, cs_ref, cext_ref, *, nb, t, d_qk, d_v, scale):
    ti = pl.program_id(1)
    d_c = ca_ref.shape[-1]
    d_g = g_ref.shape[-1]
    h = _rms(x_ref[...], gain_ref[...]).astype(BF16)

    def proj(c0, n):
        return jnp.dot(h, w_ref[:, c0:c0 + n], preferred_element_type=F32)

    @pl.when(ti == 0)
    def _():
        cext_ref[:, 0:HIST_C_PAD, :] = hc_ref[...]

    c0 = 2 * d_qk + d_v
    glu_a = proj(c0, d_c) + bglu_ref[:, :d_c]
    glu_b = proj(c0 + d_c, d_c) + bglu_ref[:, d_c:]
    cext_ref[:, HIST_C_PAD:HIST_C_PAD + t, :] = (glu_a * jax.nn.sigmoid(glu_b)).reshape(nb, t, d_c)

    q_ref[...] = (proj(0, d_qk) * scale).astype(q_ref.dtype)
    k_ref[...] = proj(d_qk, d_qk)
    v_ref[...] = proj(2 * d_qk, d_v)
    half = d_g // 2
    for i in range(2):
        gates = proj(c0 + 2 * d_c + i * half, half) + bgate_ref[:, i * half:(i + 1) * half]
        g_ref[:, i * half:(i + 1) * half] = jax.nn.sigmoid(gates).astype(g_ref.dtype)

    conv = _causal_dwconv(cext_ref, cw_ref, cb_ref[...], t=t, pad=HIST_C_PAD)
    mu = jnp.mean(conv, axis=-1, keepdims=True)
    xc = conv - mu
    var = jnp.mean(xc * xc, axis=-1, keepdims=True)
    ln = xc * lax.rsqrt(var + LN_EPS) * lng_ref[...] + lnb_ref[...]
    ca_ref[...] = jax.nn.silu(ln).reshape(nb * t, d_c).astype(ca_ref.dtype)

    tail = cext_ref[:, t:t + HIST_C_PAD, :]
    cs_ref[...] = tail
    cext_ref[:, 0:HIST_C_PAD, :] = tail


def _proj_in(x2d, hist_c, w, *, n_seq, nb, t, q_dtype, d_qk, d_v, scale):
    m, d = x2d.shape
    tiles = (m // n_seq) // t
    rows = nb * t
    n_in = w["w_in"].shape[1]
    d_c = w["conv_w"].shape[1]
    d_g = n_in - 2 * d_qk - d_v - 2 * d_c
    row = lambda n: pl.BlockSpec((rows, n), lambda b, i: (b * tiles + i, 0))
    hist = pl.BlockSpec((nb, HIST_C_PAD, d_c), lambda b, i: (b, 0, 0))
    weights = [w["norm_mix_pre"], w["w_in"], w["b_glu"], w["b_gate"]]
    conv_weights = [w["conv_w"], w["conv_b"], w["ln_g"], w["ln_b"]]
    est = (d * n_in * 2 + 2 * rows * (d * 4 + d_qk * 8 + d_v * 4 + d_c * 2 + d_g * 2)
           + nb * (HIST_C_PAD + t) * d_c * 4 + 8 * rows * max(d_qk, d_g // 2) * 4)
    return pl.pallas_call(
        functools.partial(_proj_in_kernel, nb=nb, t=t, d_qk=d_qk, d_v=d_v, scale=scale),
        grid=(n_seq // nb, tiles),
        in_specs=([row(d)] + [_resident(a.shape) for a in weights] + [hist]
                  + [_resident(a.shape) for a in conv_weights]),
        out_specs=[row(d_qk), row(d_qk), row(d_v), row(d_g), row(d_c), hist],
        out_shape=[jax.ShapeDtypeStruct((m, d_qk), q_dtype), jax.ShapeDtypeStruct((m, d_qk), F32),
                   jax.ShapeDtypeStruct((m, d_v), F32), jax.ShapeDtypeStruct((m, d_g), BF16),
                   jax.ShapeDtypeStruct((m, d_c), BF16),
                   jax.ShapeDtypeStruct((n_seq, HIST_C_PAD, d_c), F32)],
        scratch_shapes=[pltpu.VMEM((nb, HIST_C_PAD + t, d_c), F32)],
        compiler_params=pltpu.CompilerParams(
            dimension_semantics=("parallel", "arbitrary"), vmem_limit_bytes=_vmem_limit(est)),
        name="proj_in",
    )(x2d, *weights, hist_c, *conv_weights)


def _prompt_attn_kernel(lq1_ref, lk1_ref, lq2_ref, lk2_ref, subln_ref, q_ref, k_ref, v_ref, o_ref, vt_ref,
                        *, tq, head_dim, heads, lam_init):
    qi = pl.program_id(2)
    dqk = 2 * head_dim
    dv = vt_ref.shape[1]
    n_tiles = k_ref.shape[0] // tq

    @pl.when(qi == 0)
    def _():
        eye = (lax.broadcasted_iota(jnp.int32, (dv, dv), 0)
               == lax.broadcasted_iota(jnp.int32, (dv, dv), 1)).astype(BF16)
        for hh in range(heads):
            vt_ref[hh] = _dot_nt(eye, v_ref[:, hh * dv:(hh + 1) * dv].astype(BF16)).astype(BF16)

    lam = _diff_lambda(lq1_ref, lk1_ref, lq2_ref, lk2_ref, lam_init)

    def score(hh, n):
        qs = _split_halves(q_ref[:, hh * dqk:(hh + 1) * dqk], head_dim)
        kb = k_ref[0:n * tq, hh * dqk:(hh + 1) * dqk].astype(BF16)
        s_all = _dot_nt(kb, qs)
        scores, m = [], None
        for j in range(n):
            s = s_all[j * tq:(j + 1) * tq]
            if j == n - 1:
                key = lax.broadcasted_iota(jnp.int32, s.shape, 0)
                qry = lax.broadcasted_iota(jnp.int32, s.shape, 1)
                qry = jnp.where(qry >= tq, qry - tq, qry)
                s = jnp.where(key <= qry, s, -jnp.inf)
            mj = jnp.max(s, axis=0, keepdims=True)
            m = mj if m is None else jnp.maximum(m, mj)
            scores.append(s)
        return scores, m

    def finish(hh, scores, m):
        acc = jnp.zeros((dv, 2 * tq), F32)
        l = jnp.zeros((1, 2 * tq), F32)
        for j, s in enumerate(scores):
            e = jnp.exp2(s - m)
            l = l + jnp.sum(e, axis=0, keepdims=True)
            acc = acc + jnp.dot(vt_ref[hh, :, j * tq:(j + 1) * tq], e.astype(BF16),
                                preferred_element_type=F32)
        o = acc / l
        od = (o[:, :tq] - lam * o[:, tq:]).T
        o_ref[:, hh * dv:(hh + 1) * dv] = _sub_norm(od, subln_ref[...], lam_init).astype(o_ref.dtype)

    for n in range(1, n_tiles + 1):
        @pl.when(qi == n - 1)
        def _(n=n):
            pending = score(0, n)
            for hh in range(1, heads):
                upcoming = score(hh, n)
                finish(hh - 1, *pending)
                pending = upcoming
            finish(heads - 1, *pending)


def _prompt_attention(q, k, v, lam_params, subln, *, n_heads, head_dim, tq, heads, lam_init):
    b, s, _ = q.shape
    dv = v.shape[-1] // n_heads
    dqk = 2 * head_dim
    small = lambda n: pl.BlockSpec((1, n), lambda bi, h, i: (0, 0))
    est = (2 * heads * (s * (dqk + dv) * 4 + tq * (dqk + dv) * 2) + heads * dv * s * 2
           + heads * s * 2 * tq * 4 * 2)
    return pl.pallas_call(
        functools.partial(_prompt_attn_kernel, tq=tq, head_dim=head_dim, heads=heads, lam_init=lam_init),
        grid=(b, n_heads // heads, s // tq),
        in_specs=[small(head_dim)] * 4 + [small(dv),
                  pl.BlockSpec((None, tq, heads * dqk), lambda bi, h, i: (bi, i, h)),
                  pl.BlockSpec((None, s, heads * dqk), lambda bi, h, i: (bi, 0, h)),
                  pl.BlockSpec((None, s, heads * dv), lambda bi, h, i: (bi, 0, h))],
        out_specs=pl.BlockSpec((None, tq, heads * dv), lambda bi, h, i: (bi, i, h)),
        out_shape=jax.ShapeDtypeStruct((b, s, n_heads * dv), BF16),
        scratch_shapes=[pltpu.VMEM((heads, dv, s), BF16)],
        compiler_params=pltpu.CompilerParams(
            dimension_semantics=("parallel", "parallel", "arbitrary"),
            vmem_limit_bytes=_vmem_limit(est)),
        name="prompt_attn",
    )(*lam_params, subln, q, k, v)


def _sample_attn_kernel(pt_ref, lq1_ref, lk1_ref, lq2_ref, lk2_ref, subln_ref, q_ref, kn_ref, vn_ref,
                        *rest, pp, n_heads, head_dim, lam_init):
    del pt_ref
    k_refs, v_refs = rest[:pp], rest[pp:2 * pp]
    o_ref, qall_ref, vext_ref, m_ref, acc_ref = rest[2 * pp:]
    pg = pl.program_id(1)
    t = q_ref.shape[0]
    dqk = 2 * head_dim
    dv = vn_ref.shape[-1]
    n_rows = n_heads * 2 * t

    @pl.when(pg == 0)
    def _():
        qall_ref[...] = jnp.concatenate(
            [_split_halves(q_ref[:, h * dqk:(h + 1) * dqk], head_dim) for h in range(n_heads)],
            axis=0).astype(BF16)
        vext_ref[:, :, dv:] = jnp.ones((pp, vext_ref.shape[1], vext_ref.shape[2] - dv), BF16)
        m_ref[...] = jnp.full(m_ref.shape, NEG_BIG, F32)
        acc_ref[...] = jnp.zeros(acc_ref.shape, F32)

    qall = qall_ref[...]
    row = lax.broadcasted_iota(jnp.int32, (n_rows, LANES), 0)
    col = lax.broadcasted_iota(jnp.int32, (n_rows, LANES), 1)
    same_head = (col % n_heads) == (row // (2 * t))

    def fold_max(s):
        blocks = [s[:, c:c + LANES] for c in range(0, s.shape[1], LANES)]
        return functools.reduce(jnp.maximum, blocks)

    def update(scores, valid, values):
        m_old = m_ref[...]
        folded = functools.reduce(jnp.maximum, [fold_max(s) for s in scores])
        m_new = jnp.maximum(m_old, jnp.max(jnp.where(valid, folded, -jnp.inf), axis=-1, keepdims=True))
        shift = jnp.where(valid, m_new, jnp.inf)
        acc = acc_ref[...] * jnp.exp2(m_old - m_new)
        for s, v in zip(scores, values):
            e = jnp.exp2((s - jnp.tile(shift, (1, s.shape[1] // LANES))).astype(BF16))
            acc = acc + jnp.dot(e, v, preferred_element_type=F32)
        m_ref[...] = m_new
        acc_ref[...] = acc

    for i, r in enumerate(v_refs):
        vext_ref[i, :, 0:dv] = r[...].astype(BF16)
    scores = [_dot_nt(qall, r[...].astype(BF16)) for r in k_refs]
    for g in range(0, pp, PAGE_GROUP):
        update(scores[g:g + PAGE_GROUP], same_head, [vext_ref[i] for i in range(g, min(g + PAGE_GROUP, pp))])

    @pl.when(pg == pl.num_programs(1) - 1)
    def _():
        lam = _diff_lambda(lq1_ref, lk1_ref, lq2_ref, lk2_ref, lam_init)
        n_new = kn_ref.shape[0]
        pad = jnp.zeros((LANES - n_new, dqk), F32)
        kn = jnp.concatenate([kn_ref[...], pad], axis=0).astype(BF16)
        vn = jnp.concatenate([vn_ref[...], pad], axis=0).astype(BF16)
        vn = jnp.concatenate([vn, jnp.ones((LANES, acc_ref.shape[1] - dv), BF16)], axis=1)
        causal = (col // n_heads) <= (row % t)
        update([_dot_nt(qall, kn)], same_head & causal & (col < n_new), [vn])
        o = acc_ref[:, 0:dv] / acc_ref[:, dv:2 * dv]
        for h in range(n_heads):
            r0 = h * 2 * t
            od = o[r0:r0 + t] - lam * o[r0 + t:r0 + 2 * t]
            o_ref[:, h * dv:(h + 1) * dv] = _sub_norm(od, subln_ref[...], lam_init)


def _sample_attention(q, k_new, v_new, cache_k2, cache_v2, page_table, layer_base, lam_params, subln,
                      *, n_heads, head_dim, pp, lam_init):
    db, t, _ = q.shape
    n_pages = page_table.shape[1]
    rows = cache_k2.shape[1]
    dqk = 2 * head_dim
    dv = cache_v2.shape[-1]
    n_rows = n_heads * 2 * t
    small = lambda n: pl.BlockSpec((1, n), lambda b, p, pt: (0, 0))
    per_b = lambda r, n: pl.BlockSpec((None, r, n), lambda b, p, pt: (b, 0, 0))

    def page_spec(i, width):
        return pl.BlockSpec((None, rows, width), lambda b, p, pt: (layer_base + pt[b, p * pp + i], 0, 0))

    est = 2 * pp * rows * (dqk + dv) * 4 + pp * rows * 2 * dv * 2 + pp * n_rows * rows * 4 * 2
    return pl.pallas_call(
        functools.partial(_sample_attn_kernel, pp=pp, n_heads=n_heads, head_dim=head_dim,
                          lam_init=lam_init),
        grid_spec=pltpu.PrefetchScalarGridSpec(
            num_scalar_prefetch=1,
            grid=(db, n_pages // pp),
            in_specs=[small(head_dim)] * 4 + [small(dv), per_b(t, n_heads * dqk),
                                              per_b(t * n_heads, dqk), per_b(t * n_heads, dv)]
                     + [page_spec(i, dqk) for i in range(pp)] + [page_spec(i, dv) for i in range(pp)],
            out_specs=per_b(t, n_heads * dv),
            scratch_shapes=[pltpu.VMEM((n_rows, dqk), BF16), pltpu.VMEM((pp, rows, 2 * dv), BF16),
                            pltpu.VMEM((n_rows, 1), F32), pltpu.VMEM((n_rows, 2 * dv), F32)]),
        out_shape=jax.ShapeDtypeStruct((db, t, n_heads * dv), F32),
        compiler_params=pltpu.CompilerParams(
            dimension_semantics=("parallel", "arbitrary"), vmem_limit_bytes=_vmem_limit(est)),
        name="sample_attn",
    )(page_table, *lam_params, subln, q, k_new, v_new, *([cache_k2] * pp), *([cache_v2] * pp))


def _post_kernel(x_ref, o_ref, ca_ref, g_ref, hf_ref,
                 wa_ref, wb_ref, wout_ref, nmp_ref, nfp_ref, wup_ref, fcw_ref, fcb_ref, wdown_ref, nfq_ref,
                 y_ref, fs_ref, fhist_ref, *, nb, t, parts):
    ti = pl.program_id(1)
    rows = nb * t
    d = x_ref.shape[-1]
    ffn = fhist_ref.shape[-1]

    @pl.when(ti == 0)
    def _():
        fhist_ref[...] = hf_ref[...]

    def mix_stage(r):
        y_b = jnp.dot(ca_ref[r, :], wb_ref[...], preferred_element_type=F32)
        y_a = jnp.dot(o_ref[r, :].astype(BF16), wa_ref[...], preferred_element_type=F32)
        mix = g_ref[r, :d].astype(F32) * y_a + g_ref[r, d:].astype(F32) * y_b
        return x_ref[r, :] + _rms(jnp.dot(mix.astype(BF16), wout_ref[...], preferred_element_type=F32),
                                  nmp_ref[...])

    def ffn_stage(r, x1, hist):
        n = (r.stop - r.start) // nb
        h2 = _rms(x1, nfp_ref[...]).astype(BF16)
        fg = jnp.dot(h2, wup_ref[:, :ffn], preferred_element_type=F32)
        fu = jnp.dot(h2, wup_ref[:, ffn:], preferred_element_type=F32)
        fext = jnp.concatenate([hist, fg.reshape(nb, n, ffn)], axis=1)
        cv = _causal_dwconv(fext, fcw_ref, fcb_ref[...], t=n, pad=HIST_F_PAD)
        f = jax.nn.gelu(cv).reshape(nb * n, ffn) * fu
        down = jnp.dot(f.astype(BF16), wdown_ref[...], preferred_element_type=F32)
        y_ref[r, :] = x1 + _rms(down, nfq_ref[...])
        return fext[:, n:n + HIST_F_PAD, :]

    part_rows = [slice(p * (rows // parts), (p + 1) * (rows // parts)) for p in range(parts)]
    mixed = [mix_stage(r) for r in part_rows]
    tail = fhist_ref[...]
    for r, x1 in zip(part_rows, mixed):
        tail = ffn_stage(r, x1, tail)

    fs_ref[...] = tail
    fhist_ref[...] = tail


def _post(x2d, o2d, ca2d, g2d, hist_f, w, *, n_seq, nb, t):
    m, d = x2d.shape
    dv = o2d.shape[1]
    d_c = ca2d.shape[1]
    ffn = w["w_down"].shape[0]
    tiles = (m // n_seq) // t
    rows = nb * t
    row2 = lambda n: pl.BlockSpec((rows, n), lambda b, i: (b * tiles + i, 0))
    hist = pl.BlockSpec((nb, HIST_F_PAD, ffn), lambda b, i: (b, 0, 0))
    weights = [w["w_a"], w["w_b"], w["w_out"], w["norm_mix_post"], w["norm_ffn_pre"], w["w_up"],
               w["ffn_conv_w"], w["ffn_conv_b"], w["w_down"], w["norm_ffn_post"]]
    w_bytes = sum(int(a.size) * a.dtype.itemsize for a in weights)
    est = (w_bytes + 2 * rows * (d * 4 * 2 + dv * o2d.dtype.itemsize + d_c * 2 + 2 * d * 2)
           + nb * HIST_F_PAD * ffn * 4 + 10 * rows * d * 4 + 6 * rows * ffn * 4)
    return pl.pallas_call(
        functools.partial(_post_kernel, nb=nb, t=t, parts=POST_ROW_PARTS if nb == 1 else 1),
        grid=(n_seq // nb, tiles),
        in_specs=[row2(d), row2(dv), row2(d_c), row2(2 * d), hist] + [_resident(a.shape) for a in weights],
        out_specs=[row2(d), hist],
        out_shape=[jax.ShapeDtypeStruct(x2d.shape, F32), jax.ShapeDtypeStruct((n_seq, HIST_F_PAD, ffn), F32)],
        scratch_shapes=[pltpu.VMEM((nb, HIST_F_PAD, ffn), F32)],
        compiler_params=pltpu.CompilerParams(
            dimension_semantics=("parallel", "arbitrary"), vmem_limit_bytes=_vmem_limit(est)),
        name="post",
    )(x2d, o2d, ca2d, g2d, hist_f, *weights)


def _pad_history(hist, rows):
    return jnp.pad(hist, ((0, 0), (rows - hist.shape[1], 0), (0, 0)))


def kernel(x_prompt, x_sample, cache_k, cache_v, state_conv, state_ffn, page_table, norm_mix_pre, w_in, b_glu, b_gate, lambda_q1, lambda_k1, lambda_q2, lambda_k2, subln, w_a, conv_w, conv_b, ln_g, ln_b, w_b, w_out, norm_mix_post, norm_ffn_pre, w_up, ffn_conv_w, ffn_conv_b, w_down, norm_ffn_post):
    depth = w_in.shape[0]
    bp, sp, d = x_prompt.shape
    bs, ts, _ = x_sample.shape
    _, n_pool, page, n_heads, dqk = cache_k.shape
    head_dim = dqk // 2
    dv = cache_v.shape[-1]
    d_qk, d_v = n_heads * dqk, n_heads * dv
    cch = conv_w.shape[-1]
    conv_k = conv_w.shape[1]
    ffn = w_down.shape[1]
    ffn_k = ffn_conv_w.shape[1]
    scale = head_dim ** -0.5 * LOG2_E

    cache_k2 = cache_k.reshape(depth * n_pool, page * n_heads, dqk)
    cache_v2 = cache_v.reshape(depth * n_pool, page * n_heads, dv)

    yp = x_prompt.reshape(bp * sp, d)
    ys = x_sample.reshape(bs * ts, d)
    zeros_c = jnp.zeros((bp, HIST_C_PAD, cch), F32)
    zeros_f = jnp.zeros((bp, HIST_F_PAD, ffn), F32)
    outs = [[] for _ in range(8)]
    for l in range(depth):
        lam_init = 0.8 - 0.6 * math.exp(-0.3 * l)
        row = lambda a: a[l].reshape(1, -1)
        lam_params = [row(lambda_q1), row(lambda_k1), row(lambda_q2), row(lambda_k2)]
        w_in_bf = w_in[l].astype(BF16)
        w = {
            "norm_mix_pre": row(norm_mix_pre), "w_in": w_in_bf, "b_glu": row(b_glu), "b_gate": row(b_gate),
            "conv_w": conv_w[l], "conv_b": row(conv_b), "ln_g": row(ln_g), "ln_b": row(ln_b),
            "w_a": w_a[l].astype(BF16), "w_b": w_b[l].astype(BF16), "w_out": w_out[l].astype(BF16),
            "norm_mix_post": row(norm_mix_post), "norm_ffn_pre": row(norm_ffn_pre),
            "w_up": w_up[l].astype(BF16), "ffn_conv_w": ffn_conv_w[l], "ffn_conv_b": row(ffn_conv_b),
            "w_down": w_down[l].astype(BF16), "norm_ffn_post": row(norm_ffn_post),
        }
        proj_kw = dict(d_qk=d_qk, d_v=d_v, scale=scale)

        q, k, v, g, ca, cs = _proj_in(yp, zeros_c, w, n_seq=bp, nb=1, t=512, q_dtype=BF16, **proj_kw)
        o = _prompt_attention(q.reshape(bp, sp, d_qk), k.reshape(bp, sp, d_qk), v.reshape(bp, sp, d_v),
                              lam_params, row(subln), n_heads=n_heads, head_dim=head_dim, tq=512, heads=4,
                              lam_init=lam_init)
        yp, fs = _post(yp, o.reshape(bp * sp, d_v), ca, g, zeros_f, w, n_seq=bp, nb=1, t=512)
        outs[0].append(k.reshape(bp, sp, n_heads, dqk))
        outs[1].append(v.reshape(bp, sp, n_heads, dv))
        outs[2].append(cs[:, HIST_C_PAD - (conv_k - 1):])
        outs[3].append(fs[:, HIST_F_PAD - (ffn_k - 1):])

        q, k, v, g, ca, cs = _proj_in(ys, _pad_history(state_conv[l], HIST_C_PAD), w, n_seq=bs, nb=bs, t=ts,
                                      q_dtype=F32, **proj_kw)
        k_rows = k.reshape(bs, ts, n_heads, dqk)
        v_rows = v.reshape(bs, ts, n_heads, dv)
        o = _sample_attention(q.reshape(bs, ts, d_qk), k_rows.reshape(bs, ts * n_heads, dqk),
                              v_rows.reshape(bs, ts * n_heads, dv), cache_k2, cache_v2, page_table,
                              l * n_pool, lam_params, row(subln), n_heads=n_heads, head_dim=head_dim,
                              pp=16, lam_init=lam_init)
        ys, fs = _post(ys, o.reshape(bs * ts, d_v), ca, g, _pad_history(state_ffn[l], HIST_F_PAD), w,
                       n_seq=bs, nb=bs, t=ts)
        outs[4].append(k_rows)
        outs[5].append(v_rows)
        outs[6].append(cs[:, HIST_C_PAD - (conv_k - 1):])
        outs[7].append(fs[:, HIST_F_PAD - (ffn_k - 1):])

    return (yp.reshape(bp, sp, d), ys.reshape(bs, ts, d)) + tuple(jnp.stack(o) for o in outs)
```

```python
import functools
import math

import jax
import jax.numpy as jnp
from jax import lax
from jax.experimental import pallas as pl
from jax.experimental.pallas import tpu as pltpu

F32 = jnp.float32
BF16 = jnp.bfloat16

V7X_VMEM_BYTES = 64 * 1024 * 1024
LANES = 128
SUBLANES = 8

EPS = 1e-6
LN_EPS = 1e-5
NEG_BIG = -1e30

HIST_C_PAD = 32
HIST_F_PAD = 8
POST_ROW_PARTS = 2
LOG2_E = math.log2(math.e)
PAGE_GROUP = 4

def _vmem_limit(estimate_bytes):
    return int(min(V7X_VMEM_BYTES - 4 * 1024 * 1024, max(estimate_bytes, 16 * 1024 * 1024)))


def _resident(shape):
    nd = len(shape)
    return pl.BlockSpec(shape, lambda *_: (0,) * nd, pipeline_mode=pl.Buffered(1))


def _rms(x, gain):
    return x * lax.rsqrt(jnp.mean(x * x, axis=-1, keepdims=True) + EPS) * gain


def _diff_lambda(lq1_ref, lk1_ref, lq2_ref, lk2_ref, lam_init):
    a = jnp.sum(lq1_ref[...] * lk1_ref[...], axis=-1, keepdims=True)
    b = jnp.sum(lq2_ref[...] * lk2_ref[...], axis=-1, keepdims=True)
    return jnp.exp(a) - jnp.exp(b) + lam_init


def _dot_nt(a, b):
    return lax.dot_general(a, b, (((1,), (1,)), ((), ())), preferred_element_type=F32)


def _split_halves(qh, head_dim):
    lane = lax.broadcasted_iota(jnp.int32, qh.shape, 1)
    zero = jnp.zeros_like(qh)
    return jnp.concatenate([jnp.where(lane < head_dim, qh, zero),
                            jnp.where(lane >= head_dim, qh, zero)], axis=0)


def _sub_norm(o, subln, lam_init):
    return _rms(o, subln) * (1.0 - lam_init)


def _causal_dwconv(ext, w_ref, bias, *, t, pad):
    k = w_ref.shape[0]
    base = pad - (k - 1)
    y = None
    for r in range(SUBLANES):
        taps = [j for j in range(k) if (base + j) % SUBLANES == r]
        if not taps:
            continue
        span = t if r == 0 else t + SUBLANES
        u = None
        for j in taps:
            a0 = base + j - r
            term = ext[:, a0:a0 + span, :] * w_ref[pl.ds(j, 1), :]
            u = term if u is None else u + term
        u = u if r == 0 else u[:, r:r + t, :]
        y = u if y is None else y + u
    return y + bias


def _proj_in_kernel(x_ref, gain_ref, w_ref, bglu_ref, bgate_ref, hc_ref, cw_ref, cb_ref, lng_ref, lnb_ref,
                    q_ref, k_ref, v_ref, g_ref, ca_ref, cs_ref, cext_ref, *, nb, t, d_qk, d_v, scale):
    ti = pl.program_id(1)
    d_c = ca_ref.shape[-1]
    d_g = g_ref.shape[-1]
    h = _rms(x_ref[...], gain_ref[...]).astype(BF16)

    def proj(c0, n):
        return jnp.dot(h, w_ref[:, c0:c0 + n], preferred_element_type=F32)

    @pl.when(ti == 0)
    def _():
        cext_ref[:, 0:HIST_C_PAD, :] = hc_ref[...]

    c0 = 2 * d_qk + d_v
    glu_a = proj(c0, d_c) + bglu_ref[:, :d_c]
    glu_b = proj(c0 + d_c, d_c) + bglu_ref[:, d_c:]
    cext_ref[:, HIST_C_PAD:HIST_C_PAD + t, :] = (glu_a * jax.nn.sigmoid(glu_b)).reshape(nb, t, d_c)

    q_ref[...] = (proj(0, d_qk) * scale).astype(q_ref.dtype)
    k_ref[...] = proj(d_qk, d_qk)
    v_ref[...] = proj(2 * d_qk, d_v)
    half = d_g // 2
    for i in range(2):
        gates = proj(c0 + 2 * d_c + i * half, half) + bgate_ref[:, i * half:(i + 1) * half]
        g_ref[:, i * half:(i + 1) * half] = jax.nn.sigmoid(gates).astype(g_ref.dtype)

    conv = _causal_dwconv(cext_ref, cw_ref, cb_ref[...], t=t, pad=HIST_C_PAD)
    mu = jnp.mean(conv, axis=-1, keepdims=True)
    xc = conv - mu
    var = jnp.mean(xc * xc, axis=-1, keepdims=True)
    ln = xc * lax.rsqrt(var + LN_EPS) * lng_ref[...] + lnb_ref[...]
    ca_ref[...] = jax.nn.silu(ln).reshape(nb * t, d_c).astype(ca_ref.dtype)

    tail = cext_ref[:, t:t + HIST_C_PAD, :]
    cs_ref[...] = tail
    cext_ref[:, 0:HIST_C_PAD, :] = tail


def _proj_in(x2d, hist_c, w, *, n_seq, nb, t, q_dtype, d_qk, d_v, scale):
    m, d = x2d.shape
    tiles = (m // n_seq) // t
    rows = nb * t
    n_in = w["w_in"].shape[1]
    d_c = w["conv_w"].shape[1]
    d_g = n_in - 2 * d_qk - d_v - 2 * d_c
    row = lambda n: pl.BlockSpec((rows, n), lambda b, i: (b * tiles + i, 0))
    hist = pl.BlockSpec((nb, HIST_C_PAD, d_c), lambda b, i: (b, 0, 0))
    weights = [w["norm_mix_pre"], w["w_in"], w["b_glu"], w["b_gate"]]
    conv_weights = [w["conv_w"], w["conv_b"], w["ln_g"], w["ln_b"]]
    est = (d * n_in * 2 + 2 * rows * (d * 4 + d_qk * 8 + d_v * 4 + d_c * 2 + d_g * 2)
           + nb * (HIST_C_PAD + t) * d_c * 4 + 8 * rows * max(d_qk, d_g // 2) * 4)
    return pl.pallas_call(
        functools.partial(_proj_in_kernel, nb=nb, t=t, d_qk=d_qk, d_v=d_v, scale=scale),
        grid=(n_seq // nb, tiles),
        in_specs=([row(d)] + [_resident(a.shape) for a in weights] + [hist]
                  + [_resident(a.shape) for a in conv_weights]),
        out_specs=[row(d_qk), row(d_qk), row(d_v), row(d_g), row(d_c), hist],
        out_shape=[jax.ShapeDtypeStruct((m, d_qk), q_dtype), jax.ShapeDtypeStruct((m, d_qk), F32),
                   jax.ShapeDtypeStruct((m, d_v), F32), jax.ShapeDtypeStruct((m, d_g), BF16),
                   jax.ShapeDtypeStruct((m, d_c), BF16),
                   jax.ShapeDtypeStruct((n_seq, HIST_C_PAD, d_c), F32)],
        scratch_shapes=[pltpu.VMEM((nb, HIST_C_PAD + t, d_c), F32)],
        compiler_params=pltpu.CompilerParams(
            dimension_semantics=("parallel", "arbitrary"), vmem_limit_bytes=_vmem_limit(est)),
        name="proj_in",
    )(x2d, *weights, hist_c, *conv_weights)


def _prompt_attn_kernel(lq1_ref, lk1_ref, lq2_ref, lk2_ref, subln_ref, q_ref, k_ref, v_ref, o_ref, vt_ref,
                        *, tq, head_dim, heads, lam_init):
    qi = pl.program_id(2)
    dqk = 2 * head_dim
    dv = vt_ref.shape[1]
    n_tiles = k_ref.shape[0] // tq

    @pl.when(qi == 0)
    def _():
        for hh in range(heads):
            vt_ref[hh] = v_ref[:, hh * dv:(hh + 1) * dv].T.astype(BF16)

    lam = _diff_lambda(lq1_ref, lk1_ref, lq2_ref, lk2_ref, lam_init)

    def score(hh, n):
        qs = _split_halves(q_ref[:, hh * dqk:(hh + 1) * dqk], head_dim)
        kb = k_ref[0:n * tq, hh * dqk:(hh + 1) * dqk].astype(BF16)
        s_all = _dot_nt(kb, qs)
        scores, m = [], None
        for j in range(n):
            s = s_all[j * tq:(j + 1) * tq]
            if j == n - 1:
                key = lax.broadcasted_iota(jnp.int32, s.shape, 0)
                qry = lax.broadcasted_iota(jnp.int32, s.shape, 1)
                qry = jnp.where(qry >= tq, qry - tq, qry)
                s = jnp.where(key <= qry, s, -jnp.inf)
            mj = jnp.max(s, axis=0, keepdims=True)
            m = mj if m is None else jnp.maximum(m, mj)
            scores.append(s)
        return scores, m

    def finish(hh, scores, m):
        acc = jnp.zeros((dv, 2 * tq), F32)
        l = jnp.zeros((1, 2 * tq), F32)
        for j, s in enumerate(scores):
            e = jnp.exp2(s - m)
            l = l + jnp.sum(e, axis=0, keepdims=True)
            acc = acc + jnp.dot(vt_ref[hh, :, j * tq:(j + 1) * tq], e.astype(BF16),
                                preferred_element_type=F32)
        o = acc / l
        od = (o[:, :tq] - lam * o[:, tq:]).T
        o_ref[:, hh * dv:(hh + 1) * dv] = _sub_norm(od, subln_ref[...], lam_init).astype(o_ref.dtype)

    for n in range(1, n_tiles + 1):
        @pl.when(qi == n - 1)
        def _(n=n):
            pending = score(0, n)
            for hh in range(1, heads):
                upcoming = score(hh, n)
                finish(hh - 1, *pending)
                pending = upcoming
            finish(heads - 1, *pending)


def _prompt_attention(q, k, v, lam_params, subln, *, n_heads, head_dim, tq, heads, lam_init):
    b, s, _ = q.shape
    dv = v.shape[-1] // n_heads
    dqk = 2 * head_dim
    small = lambda n: pl.BlockSpec((1, n), lambda bi, h, i: (0, 0))
    est = (2 * heads * (s * (dqk + dv) * 4 + tq * (dqk + dv) * 2) + heads * dv * s * 2
           + heads * s * 2 * tq * 4 * 2)
    return pl.pallas_call(
        functools.partial(_prompt_attn_kernel, tq=tq, head_dim=head_dim, heads=heads, lam_init=lam_init),
        grid=(b, n_heads // heads, s // tq),
        in_specs=[small(head_dim)] * 4 + [small(dv),
                  pl.BlockSpec((None, tq, heads * dqk), lambda bi, h, i: (bi, i, h)),
                  pl.BlockSpec((None, s, heads * dqk), lambda bi, h, i: (bi, 0, h)),
                  pl.BlockSpec((None, s, heads * dv), lambda bi, h, i: (bi, 0, h))],
        out_specs=pl.BlockSpec((None, tq, heads * dv), lambda bi, h, i: (bi, i, h)),
        out_shape=jax.ShapeDtypeStruct((b, s, n_heads * dv), BF16),
        scratch_shapes=[pltpu.VMEM((heads, dv, s), BF16)],
        compiler_params=pltpu.CompilerParams(
            dimension_semantics=("parallel", "parallel", "arbitrary"),
            vmem_limit_bytes=_vmem_limit(est)),
        name="prompt_attn",
    )(*lam_params, subln, q, k, v)


def _sample_attn_kernel(pt_ref, lq1_ref, lk1_ref, lq2_ref, lk2_ref, subln_ref, q_ref, kn_ref, vn_ref,
                        *rest, pp, n_heads, head_dim, lam_init):
    del pt_ref
    k_refs, v_refs = rest[:pp], rest[pp:2 * pp]
    o_ref, qall_ref, vext_ref, m_ref, acc_ref = rest[2 * pp:]
    pg = pl.program_id(1)
    t = q_ref.shape[0]
    dqk = 2 * head_dim
    dv = vn_ref.shape[-1]
    n_rows = n_heads * 2 * t

    @pl.when(pg == 0)
    def _():
        qall_ref[...] = jnp.concatenate(
            [_split_halves(q_ref[:, h * dqk:(h + 1) * dqk], head_dim) for h in range(n_heads)],
            axis=0).astype(BF16)
        vext_ref[:, :, dv:] = jnp.ones((pp, vext_ref.shape[1], vext_ref.shape[2] - dv), BF16)
        m_ref[...] = jnp.full(m_ref.shape, NEG_BIG, F32)
        acc_ref[...] = jnp.zeros(acc_ref.shape, F32)

    qall = qall_ref[...]
    row = lax.broadcasted_iota(jnp.int32, (n_rows, LANES), 0)
    col = lax.broadcasted_iota(jnp.int32, (n_rows, LANES), 1)
    same_head = (col % n_heads) == (row // (2 * t))

    def fold_max(s):
        blocks = [s[:, c:c + LANES] for c in range(0, s.shape[1], LANES)]
        return functools.reduce(jnp.maximum, blocks)

    def update(scores, valid, values):
        m_old = m_ref[...]
        folded = functools.reduce(jnp.maximum, [fold_max(s) for s in scores])
        m_new = jnp.maximum(m_old, jnp.max(jnp.where(valid, folded, -jnp.inf), axis=-1, keepdims=True))
        shift = jnp.where(valid, m_new, jnp.inf)
        acc = acc_ref[...] * jnp.exp2(m_old - m_new)
        for s, v in zip(scores, values):
            e = jnp.exp2((s - jnp.tile(shift, (1, s.shape[1] // LANES))).astype(BF16))
            acc = acc + jnp.dot(e, v, preferred_element_type=F32)
        m_ref[...] = m_new
        acc_ref[...] = acc

    for i, r in enumerate(v_refs):
        vext_ref[i, :, 0:dv] = r[...].astype(BF16)
    scores = [_dot_nt(qall, r[...].astype(BF16)) for r in k_refs]
    for g in range(0, pp, PAGE_GROUP):
        update(scores[g:g + PAGE_GROUP], same_head, [vext_ref[i] for i in range(g, min(g + PAGE_GROUP, pp))])

    @pl.when(pg == pl.num_programs(1) - 1)
    def _():
        lam = _diff_lambda(lq1_ref, lk1_ref, lq2_ref, lk2_ref, lam_init)
        n_new = kn_ref.shape[0]
        pad = jnp.zeros((LANES - n_new, dqk), F32)
        kn = jnp.concatenate([kn_ref[...], pad], axis=0).astype(BF16)
        vn = jnp.concatenate([vn_ref[...], pad], axis=0).astype(BF16)
        vn = jnp.concatenate([vn, jnp.ones((LANES, acc_ref.shape[1] - dv), BF16)], axis=1)
        causal = (col // n_heads) <= (row % t)
        update([_dot_nt(qall, kn)], same_head & causal & (col < n_new), [vn])
        o = acc_ref[:, 0:dv] / acc_ref[:, dv:2 * dv]
        for h in range(n_heads):
            r0 = h * 2 * t
            od = o[r0:r0 + t] - lam * o[r0 + t:r0 + 2 * t]
            o_ref[:, h * dv:(h + 1) * dv] = _sub_norm(od, subln_ref[...], lam_init)


def _sample_attention(q, k_new, v_new, cache_k2, cache_v2, page_table, layer_base, lam_params, subln,
                      *, n_heads, head_dim, pp, lam_init):
    db, t, _ = q.shape
    n_pages = page_table.shape[1]
    rows = cache_k2.shape[1]
    dqk = 2 * head_dim
    dv = cache_v2.shape[-1]
    n_rows = n_heads * 2 * t
    small = lambda n: pl.BlockSpec((1, n), lambda b, p, pt: (0, 0))
    per_b = lambda r, n: pl.BlockSpec((None, r, n), lambda b, p, pt: (b, 0, 0))

    def page_spec(i, width):
        return pl.BlockSpec((None, rows, width), lambda b, p, pt: (layer_base + pt[b, p * pp + i], 0, 0))

    est = 2 * pp * rows * (dqk + dv) * 4 + pp * rows * 2 * dv * 2 + pp * n_rows * rows * 4 * 2
    return pl.pallas_call(
        functools.partial(_sample_attn_kernel, pp=pp, n_heads=n_heads, head_dim=head_dim,
                          lam_init=lam_init),
        grid_spec=pltpu.PrefetchScalarGridSpec(
            num_scalar_prefetch=1,
            grid=(db, n_pages // pp),
            in_specs=[small(head_dim)] * 4 + [small(dv), per_b(t, n_heads * dqk),
                                              per_b(t * n_heads, dqk), per_b(t * n_heads, dv)]
                     + [page_spec(i, dqk) for i in range(pp)] + [page_spec(i, dv) for i in range(pp)],
            out_specs=per_b(t, n_heads * dv),
            scratch_shapes=[pltpu.VMEM((n_rows, dqk), BF16), pltpu.VMEM((pp, rows, 2 * dv), BF16),
                            pltpu.VMEM((n_rows, 1), F32), pltpu.VMEM((n_rows, 2 * dv), F32)]),
        out_shape=jax.ShapeDtypeStruct((db, t, n_heads * dv), F32),
        compiler_params=pltpu.CompilerParams(
            dimension_semantics=("parallel", "arbitrary"), vmem_limit_bytes=_vmem_limit(est)),
        name="sample_attn",
    )(page_table, *lam_params, subln, q, k_new, v_new, *([cache_k2] * pp), *([cache_v2] * pp))


def _post_kernel(x_ref, o_ref, ca_ref, g_ref, hf_ref,
                 wa_ref, wb_ref, wout_ref, nmp_ref, nfp_ref, wup_ref, fcw_ref, fcb_ref, wdown_ref, nfq_ref,
                 y_ref, fs_ref, fhist_ref, *, nb, t, parts):
    ti = pl.program_id(1)
    rows = nb * t
    d = x_ref.shape[-1]
    ffn = fhist_ref.shape[-1]

    @pl.when(ti == 0)
    def _():
        fhist_ref[...] = hf_ref[...]

    def mix_stage(r):
        y_b = jnp.dot(ca_ref[r, :], wb_ref[...], preferred_element_type=F32)
        y_a = jnp.dot(o_ref[r, :].astype(BF16), wa_ref[...], preferred_element_type=F32)
        mix = g_ref[r, :d].astype(F32) * y_a + g_ref[r, d:].astype(F32) * y_b
        return x_ref[r, :] + _rms(jnp.dot(mix.astype(BF16), wout_ref[...], preferred_element_type=F32),
                                  nmp_ref[...])

    def ffn_stage(r, x1, hist):
        n = (r.stop - r.start) // nb
        h2 = _rms(x1, nfp_ref[...]).astype(BF16)
        fg = jnp.dot(h2, wup_ref[:, :ffn], preferred_element_type=F32)
        fu = jnp.dot(h2, wup_ref[:, ffn:], preferred_element_type=F32)
        fext = jnp.concatenate([hist, fg.reshape(nb, n, ffn)], axis=1)
        cv = _causal_dwconv(fext, fcw_ref, fcb_ref[...], t=n, pad=HIST_F_PAD)
        f = jax.nn.gelu(cv).reshape(nb * n, ffn) * fu
        down = jnp.dot(f.astype(BF16), wdown_ref[...], preferred_element_type=F32)
        y_ref[r, :] = x1 + _rms(down, nfq_ref[...])
        return fext[:, n:n + HIST_F_PAD, :]

    part_rows = [slice(p * (rows // parts), (p + 1) * (rows // parts)) for p in range(parts)]
    mixed = [mix_stage(r) for r in part_rows]
    tail = fhist_ref[...]
    for r, x1 in zip(part_rows, mixed):
        tail = ffn_stage(r, x1, tail)

    fs_ref[...] = tail
    fhist_ref[...] = tail


def _post(x2d, o2d, ca2d, g2d, hist_f, w, *, n_seq, nb, t):
    m, d = x2d.shape
    dv = o2d.shape[1]
    d_c = ca2d.shape[1]
    ffn = w["w_down"].shape[0]
    tiles = (m // n_seq) // t
    rows = nb * t
    row2 = lambda n: pl.BlockSpec((rows, n), lambda b, i: (b * tiles + i, 0))
    hist = pl.BlockSpec((nb, HIST_F_PAD, ffn), lambda b, i: (b, 0, 0))
    weights = [w["w_a"], w["w_b"], w["w_out"], w["norm_mix_post"], w["norm_ffn_pre"], w["w_up"],
               w["ffn_conv_w"], w["ffn_conv_b"], w["w_down"], w["norm_ffn_post"]]
    w_bytes = sum(int(a.size) * a.dtype.itemsize for a in weights)
    est = (w_bytes + 2 * rows * (d * 4 * 2 + dv * o2d.dtype.itemsize + d_c * 2 + 2 * d * 2)
           + nb * HIST_F_PAD * ffn * 4 + 10 * rows * d * 4 + 6 * rows * ffn * 4)
    return pl.pallas_call(
        functools.partial(_post_kernel, nb=nb, t=t, parts=POST_ROW_PARTS if nb == 1 else 1),
        grid=(n_seq // nb, tiles),
        in_specs=[row2(d), row2(dv), row2(d_c), row2(2 * d), hist] + [_resident(a.shape) for a in weights],
        out_specs=[row2(d), hist],
        out_shape=[jax.ShapeDtypeStruct(x2d.shape, F32), jax.ShapeDtypeStruct((n_seq, HIST_F_PAD, ffn), F32)],
        scratch_shapes=[pltpu.VMEM((nb, HIST_F_PAD, ffn), F32)],
        compiler_params=pltpu.CompilerParams(
            dimension_semantics=("parallel", "arbitrary"), vmem_limit_bytes=_vmem_limit(est)),
        name="post",
    )(x2d, o2d, ca2d, g2d, hist_f, *weights)


def _pad_history(hist, rows):
    return jnp.pad(hist, ((0, 0), (rows - hist.shape[1], 0), (0, 0)))


def kernel(x_prompt, x_sample, cache_k, cache_v, state_conv, state_ffn, page_table, norm_mix_pre, w_in, b_glu, b_gate, lambda_q1, lambda_k1, lambda_q2, lambda_k2, subln, w_a, conv_w, conv_b, ln_g, ln_b, w_b, w_out, norm_mix_post, norm_ffn_pre, w_up, ffn_conv_w, ffn_conv_b, w_down, norm_ffn_post):
    depth = w_in.shape[0]
    bp, sp, d = x_prompt.shape
    bs, ts, _ = x_sample.shape
    _, n_pool, page, n_heads, dqk = cache_k.shape
    head_dim = dqk // 2
    dv = cache_v.shape[-1]
    d_qk, d_v = n_heads * dqk, n_heads * dv
    cch = conv_w.shape[-1]
    conv_k = conv_w.shape[1]
    ffn = w_down.shape[1]
    ffn_k = ffn_conv_w.shape[1]
    scale = head_dim ** -0.5 * LOG2_E

    cache_k2 = cache_k.reshape(depth * n_pool, page * n_heads, dqk)
    cache_v2 = cache_v.reshape(depth * n_pool, page * n_heads, dv)

    yp = x_prompt.reshape(bp * sp, d)
    ys = x_sample.reshape(bs * ts, d)
    zeros_c = jnp.zeros((bp, HIST_C_PAD, cch), F32)
    zeros_f = jnp.zeros((bp, HIST_F_PAD, ffn), F32)
    outs = [[] for _ in range(8)]
    for l in range(depth):
        lam_init = 0.8 - 0.6 * math.exp(-0.3 * l)
        row = lambda a: a[l].reshape(1, -1)
        lam_params = [row(lambda_q1), row(lambda_k1), row(lambda_q2), row(lambda_k2)]
        w_in_bf = w_in[l].astype(BF16)
        w = {
            "norm_mix_pre": row(norm_mix_pre), "w_in": w_in_bf, "b_glu": row(b_glu), "b_gate": row(b_gate),
            "conv_w": conv_w[l], "conv_b": row(conv_b), "ln_g": row(ln_g), "ln_b": row(ln_b),
            "w_a": w_a[l].astype(BF16), "w_b": w_b[l].astype(BF16), "w_out": w_out[l].astype(BF16),
            "norm_mix_post": row(norm_mix_post), "norm_ffn_pre": row(norm_ffn_pre),
            "w_up": w_up[l].astype(BF16), "ffn_conv_w": ffn_conv_w[l], "ffn_conv_b": row(ffn_conv_b),
            "w_down": w_down[l].astype(BF16), "norm_ffn_post": row(norm_ffn_post),
        }
        proj_kw = dict(d_qk=d_qk, d_v=d_v, scale=scale)

        q, k, v, g, ca, cs = _proj_in(yp, zeros_c, w, n_seq=bp, nb=1, t=512, q_dtype=BF16, **proj_kw)
        o = _prompt_attention(q.reshape(bp, sp, d_qk), k.reshape(bp, sp, d_qk), v.reshape(bp, sp, d_v),
                              lam_params, row(subln), n_heads=n_heads, head_dim=head_dim, tq=512, heads=4,
                              lam_init=lam_init)
        yp, fs = _post(yp, o.reshape(bp * sp, d_v), ca, g, zeros_f, w, n_seq=bp, nb=1, t=512)
        outs[0].append(k.reshape(bp, sp, n_heads, dqk))
        outs[1].append(v.reshape(bp, sp, n_heads, dv))
        outs[2].append(cs[:, HIST_C_PAD - (conv_k - 1):])
        outs[3].append(fs[:, HIST_F_PAD - (ffn_k - 1):])

        q, k, v, g, ca, cs = _proj_in(ys, _pad_history(state_conv[l], HIST_C_PAD), w, n_seq=bs, nb=bs, t=ts,
                                      q_dtype=F32, **proj_kw)
        k_rows = k.reshape(bs, ts, n_heads, dqk)
        v_rows = v.reshape(bs, ts, n_heads, dv)
        o = _sample_attention(q.reshape(bs, ts, d_qk), k_rows.reshape(bs, ts * n_heads, dqk),
                              v_rows.reshape(bs, ts * n_heads, dv), cache_k2, cache_v2, page_table,
                              l * n_pool, lam_params, row(subln), n_heads=n_heads, head_dim=head_dim,
                              pp=16, lam_init=lam_init)
        ys, fs = _post(ys, o.reshape(bs * ts, d_v), ca, g, _pad_history(state_ffn[l], HIST_F_PAD), w,
                       n_seq=bs, nb=bs, t=ts)
        outs[4].append(k_rows)
        outs[5].append(v_rows)
        outs[6].append(cs[:, HIST_C_PAD - (conv_k - 1):])
        outs[7].append(fs[:, HIST_F_PAD - (ffn_k - 1):])

    return (yp.reshape(bp, sp, d), ys.reshape(bs, ts, d)) + tuple(jnp.stack(o) for o in outs)
```
